```python
import math
import jax, jax.numpy as jnp
from jax import lax
import numpy as np

D_MODEL = 1024
BATCH = 1
SEQ = 16384
DEPTH = 4

GRID_W = 64
CTX_LEN = 256
N_MIXERS = 3
N_A = (DEPTH + 2) // 3
N_B = (DEPTH + 1) // 3
N_C = DEPTH // 3

SSM_EXPAND = 2
SSM_D_INNER = SSM_EXPAND * D_MODEL
SSM_HEAD_DIM = 64
SSM_HEADS = SSM_D_INNER // SSM_HEAD_DIM
SSM_STATE = 128
SSM_GROUPS = 8
SSM_CONV = 5
SSM_CHUNK = 128
SSM_BC_DIM = SSM_GROUPS * SSM_STATE
SSM_CONV_DIM = SSM_D_INNER + 2 * SSM_BC_DIM
SSM_IN_DIM = SSM_D_INNER + SSM_CONV_DIM + 2 * SSM_HEADS

ATTN_HEADS = 16
ATTN_KV_HEADS = 4
ATTN_HEAD_DIM = 64
ATTN_WIDTH = ATTN_HEADS * ATTN_HEAD_DIM
ATTN_KV_WIDTH = ATTN_KV_HEADS * ATTN_HEAD_DIM
ATTN_IN_DIM = ATTN_WIDTH + 2 * ATTN_KV_WIDTH + ATTN_WIDTH
ATTN_Q_BLOCK = 128
ROPE_THETA = 10000.0

POOL_WIDTH = 2 * D_MODEL
POOL_WINDOWS = (2, 4, 8, 16)
POOL_GROUPS = len(POOL_WINDOWS)
POOL_GROUP_DIM = POOL_WIDTH // POOL_GROUPS

DEEPNORM_ALPHA = (2 * DEPTH) ** 0.25
DEEPNORM_BETA = (8 * DEPTH) ** -0.25
LN_EPS = 1e-5
RMS_EPS = 1e-6

kernel_name = "hybrid_ssd_gqa_pool_flow_backbone"

F32 = jnp.float32


def _layer_norm(x, g, b):
    xf = x.astype(F32)
    mu = jnp.mean(xf, axis=-1, keepdims=True)
    var = jnp.mean(jnp.square(xf - mu), axis=-1, keepdims=True)
    return ((xf - mu) * lax.rsqrt(var + LN_EPS)).astype(x.dtype) * g + b


def _rms_norm(x, g):
    xf = x.astype(F32)
    return (xf * lax.rsqrt(jnp.mean(xf * xf, axis=-1, keepdims=True) + RMS_EPS)).astype(x.dtype) * g


def _dwconv_centred(u, w, b):
    pad = w.shape[0] // 2
    out = lax.conv_general_dilated(u, w[:, None, :], window_strides=(1,), padding=[(pad, pad)],
                                   dimension_numbers=('NWC', 'WIO', 'NWC'),
                                   feature_group_count=u.shape[-1])
    return out + b


def _axial_rope(n_tokens, dtype):
    rows = n_tokens // GRID_W
    row_ids = jnp.repeat(jnp.arange(rows), GRID_W).astype(F32)
    col_ids = jnp.tile(jnp.arange(GRID_W), rows).astype(F32)
    half = ATTN_HEAD_DIM // 2
    inv = ROPE_THETA ** (-jnp.arange(0, half, 2, dtype=F32) / half)
    ang = jnp.concatenate([row_ids[:, None] * inv, col_ids[:, None] * inv], axis=-1)
    return jnp.cos(ang).astype(dtype), jnp.sin(ang).astype(dtype)


def _apply_rope(x, cos, sin):
    x1, x2 = jnp.split(x, 2, axis=-1)
    c = cos[None, :, None, :]
    s = sin[None, :, None, :]
    return jnp.concatenate([x1 * c - x2 * s, x2 * c + x1 * s], axis=-1)


def _ssd_scan(xh, dt, a, bm, cm, h0):
    bsz, n_tok, n_heads, p_dim = xh.shape
    n_chunks = n_tok // SSM_CHUNK
    rep = n_heads // bm.shape[2]

    def to_chunks(t):
        return jnp.moveaxis(t.reshape((bsz, n_chunks, SSM_CHUNK) + t.shape[2:]), 1, 0)

    xs = (to_chunks(xh.astype(F32)), to_chunks(dt), to_chunks(bm.astype(F32)), to_chunks(cm.astype(F32)))
    lower = jnp.tril(jnp.ones((SSM_CHUNK, SSM_CHUNK), dtype=bool))

    def step(h, inp):
        xc, dtc, bc, cc = inp
        bc = jnp.repeat(bc, rep, axis=2)
        cc = jnp.repeat(cc, rep, axis=2)
        cum = jnp.cumsum(dtc * a, axis=1)
        seg = cum[:, :, None, :] - cum[:, None, :, :]
        decay = jnp.exp(jnp.where(lower[None, :, :, None], seg, -jnp.inf))
        xdt = xc * dtc[..., None]
        scores = jnp.einsum('blhn,bshn->blsh', cc, bc) * decay
        y = jnp.einsum('blsh,bshp->blhp', scores, xdt)
        y = y + jnp.einsum('blhn,bhpn->blhp', cc * jnp.exp(cum)[..., None], h)
        tail = jnp.exp(cum[:, -1:, :] - cum)
        h_new = h * jnp.exp(cum[:, -1, :])[:, :, None, None] + \
            jnp.einsum('bshn,bshp->bhpn', bc * tail[..., None], xdt)
        return h_new, y

    h_final, ys = lax.scan(step, h0, xs)
    y = jnp.moveaxis(ys, 0, 1).reshape(bsz, n_tok, n_heads, p_dim)
    return y, h_final


def _mamba_mixer(hx, hc, w_in, conv_w, conv_b, dt_bias, a_log, d_skip, norm_g, w_out, compute_ctx):
    def project(h):
        bsz, n_tok = h.shape[:2]
        zxbcdt = h @ w_in
        z, xbc, dt = jnp.split(zxbcdt, [SSM_D_INNER, SSM_D_INNER + SSM_CONV_DIM], axis=-1)
        xbc = jax.nn.silu(_dwconv_centred(xbc, conv_w, conv_b))
        xs, bm, cm = jnp.split(xbc, [SSM_D_INNER, SSM_D_INNER + SSM_BC_DIM], axis=-1)
        xs = xs.reshape(bsz, n_tok, SSM_HEADS, SSM_HEAD_DIM)
        bm = bm.reshape(bsz, n_tok, SSM_GROUPS, SSM_STATE)
        cm = cm.reshape(bsz, n_tok, SSM_GROUPS, SSM_STATE)
        dt = jax.nn.softplus(dt.reshape(bsz, n_tok, 2, SSM_HEADS).astype(F32) + dt_bias.astype(F32))
        return z, xs, bm, cm, dt

    a = -jnp.exp(a_log.astype(F32))
    zc, xc, bc, cc, dtc = project(hc)
    zx, xx, bx, cx, dtx = project(hx)
    h0 = jnp.zeros((hx.shape[0], SSM_HEADS, SSM_HEAD_DIM, SSM_STATE), F32)
    flip = lambda t: jnp.flip(t, axis=1)
    yc_f, sc_f = _ssd_scan(xc, dtc[:, :, 0], a[0], bc, cc, h0)
    yx_f, _ = _ssd_scan(xx, dtx[:, :, 0], a[0], bx, cx, sc_f)
    yc_b, sc_b = _ssd_scan(flip(xc), flip(dtc[:, :, 1]), a[1], flip(bc), flip(cc), h0)
    yx_b, _ = _ssd_scan(flip(xx), flip(dtx[:, :, 1]), a[1], flip(bx), flip(cx), sc_b)

    def finish(y_f, y_b_rev, xs, z):
        bsz, n_tok = xs.shape[:2]
        y = (y_f + flip(y_b_rev)).astype(xs.dtype) + xs * d_skip[:, None]
        y = y.reshape(bsz, n_tok, SSM_D_INNER)
        return _rms_norm(y * jax.nn.silu(z), norm_g) @ w_out

    out_x = finish(yx_f, yx_b, xx, zx)
    out_c = finish(yc_f, yc_b, xc, zc) if compute_ctx else None
    return out_x, out_c


def _attn_mixer(hx, hc, w_in, q_norm_g, k_norm_g, w_out, cos, sin, compute_ctx):
    n_rep = ATTN_HEADS // ATTN_KV_HEADS
    scale = ATTN_HEAD_DIM ** -0.5

    def project(h):
        bsz, n_tok = h.shape[:2]
        q, k, v, g = jnp.split(h @ w_in, [ATTN_WIDTH, ATTN_WIDTH + ATTN_KV_WIDTH,
                                          ATTN_WIDTH + 2 * ATTN_KV_WIDTH], axis=-1)
        q = _rms_norm(q.reshape(bsz, n_tok, ATTN_HEADS, ATTN_HEAD_DIM), q_norm_g)
        k = _rms_norm(k.reshape(bsz, n_tok, ATTN_KV_HEADS, ATTN_HEAD_DIM), k_norm_g)
        v = v.reshape(bsz, n_tok, ATTN_KV_HEADS, ATTN_HEAD_DIM)
        return q, k, v, g

    def attend(q, k, v):
        bsz, nq = q.shape[:2]
        qg = q.reshape(bsz, nq, ATTN_KV_HEADS, n_rep, ATTN_HEAD_DIM)
        s = jnp.einsum('bqkgd,bskd->bkgqs', qg, k).astype(F32) * scale
        p = jax.nn.softmax(s, axis=-1).astype(v.dtype)
        o = jnp.einsum('bkgqs,bskd->bqkgd', p, v)
        return o.reshape(bsz, nq, ATTN_WIDTH)

    qc, kc, vc, gc = project(hc)
    qx, kx, vx, gx = project(hx)
    qx = _apply_rope(qx, cos, sin)
    kx = _apply_rope(kx, cos, sin)
    k_all = jnp.concatenate([kc, kx], axis=1)
    v_all = jnp.concatenate([vc, vx], axis=1)
    bsz, n_tok = hx.shape[:2]
    n_blk = n_tok // ATTN_Q_BLOCK
    q_blocks = jnp.moveaxis(qx.reshape(bsz, n_blk, ATTN_Q_BLOCK, ATTN_HEADS, ATTN_HEAD_DIM), 1, 0)
    ox = lax.map(lambda qb: attend(qb, k_all, v_all), q_blocks)
    ox = jnp.moveaxis(ox, 0, 1).reshape(bsz, n_tok, ATTN_WIDTH)
    out_x = (ox * jax.nn.silu(gx)) @ w_out
    out_c = (attend(qc, kc, vc) * jax.nn.silu(gc)) @ w_out if compute_ctx else None
    return out_x, out_c


def _centred_mean_minus_self(u, window):
    bsz, n_tok, ch = u.shape
    lo_off = window // 2
    hi_off = window - 1 - lo_off
    cs = jnp.concatenate([jnp.zeros((bsz, 1, ch), F32), jnp.cumsum(u.astype(F32), axis=1)], axis=1)
    t = jnp.arange(n_tok)
    lo = jnp.clip(t - lo_off, 0, n_tok)
    hi = jnp.clip(t + hi_off + 1, 0, n_tok)
    win_sum = jnp.take(cs, hi, axis=1) - jnp.take(cs, lo, axis=1)
    cnt = (hi - lo).astype(F32)
    return (win_sum / cnt[None, :, None]).astype(u.dtype) - u


def _pool_mixer(hx, hc, w_in, group_w, layer_scale, w_out, compute_ctx):
    def branch(h):
        bsz, n_tok = h.shape[:2]
        u, z = jnp.split(h @ w_in, 2, axis=-1)
        parts = jnp.split(u, POOL_GROUPS, axis=-1)
        pooled = jnp.stack([_centred_mean_minus_self(p, w) for p, w in zip(parts, POOL_WINDOWS)],
                           axis=-2)
        mixed = jnp.einsum('btgi,gio->btgo', pooled, group_w).reshape(bsz, n_tok, POOL_WIDTH)
        return ((mixed * layer_scale) * jax.nn.silu(z)) @ w_out

    out_x = branch(hx)
    out_c = branch(hc) if compute_ctx else None
    return out_x, out_c


def setup_inputs(seed: int = 0) -> dict:
    key = jax.random.key(seed)
    ks = jax.random.split(key, 26)
    nrm = lambda k, shape, s: jax.random.normal(k, shape, F32) * s
    dt0 = jnp.exp(jax.random.uniform(ks[10], (N_A, 2, SSM_HEADS), F32, math.log(1e-3), math.log(1e-1)))
    return {
        "x": nrm(ks[0], (BATCH, SEQ, D_MODEL), 1.0),
        "c": nrm(ks[1], (BATCH, D_MODEL), 1.0),
        "ctx": nrm(ks[2], (BATCH, CTX_LEN, D_MODEL), 1.0),
        "c_ctx": nrm(ks[3], (D_MODEL,), 1.0),
        "mod_w": nrm(ks[4], (DEPTH, D_MODEL, 3 * D_MODEL), 0.5 * D_MODEL ** -0.5),
        "mod_b": nrm(ks[5], (DEPTH, 3 * D_MODEL), 0.02),
        "ln_g": 1.0 + nrm(ks[6], (DEPTH, D_MODEL), 0.02),
        "ln_b": nrm(ks[7], (DEPTH, D_MODEL), 0.02),
        "ssm_w_in": nrm(ks[8], (N_A, D_MODEL, SSM_IN_DIM), D_MODEL ** -0.5),
        "ssm_conv_w": nrm(ks[9], (N_A, SSM_CONV, SSM_CONV_DIM), SSM_CONV ** -0.5),
        "ssm_conv_b": nrm(ks[11], (N_A, SSM_CONV_DIM), 0.02),
        "ssm_dt_bias": dt0 + jnp.log(-jnp.expm1(-dt0)),
        "ssm_a_log": jnp.log(jax.random.uniform(ks[12], (N_A, 2, SSM_HEADS), F32, 1.0, 16.0)),
        "ssm_d": 1.0 + nrm(ks[13], (N_A, SSM_HEADS), 0.1),
        "ssm_norm_g": 1.0 + nrm(ks[14], (N_A, SSM_D_INNER), 0.02),
        "ssm_w_out": nrm(ks[15], (N_A, SSM_D_INNER, D_MODEL), DEEPNORM_BETA * SSM_D_INNER ** -0.5),
        "attn_w_in": nrm(ks[16], (N_B, D_MODEL, ATTN_IN_DIM), D_MODEL ** -0.5),
        "attn_q_norm": 1.0 + nrm(ks[17], (N_B, ATTN_HEAD_DIM), 0.02),
        "attn_k_norm": 1.0 + nrm(ks[18], (N_B, ATTN_HEAD_DIM), 0.02),
        "attn_w_out": nrm(ks[19], (N_B, ATTN_WIDTH, D_MODEL), DEEPNORM_BETA * ATTN_WIDTH ** -0.5),
        "pool_w_in": nrm(ks[20], (N_C, D_MODEL, 2 * POOL_WIDTH), D_MODEL ** -0.5),
        "pool_group_w": nrm(ks[21], (N_C, POOL_GROUPS, POOL_GROUP_DIM, POOL_GROUP_DIM), POOL_GROUP_DIM ** -0.5),
        "pool_scale": 1.0 + nrm(ks[22], (N_C, POOL_WIDTH), 0.05),
        "pool_w_out": nrm(ks[23], (N_C, POOL_WIDTH, D_MODEL), DEEPNORM_BETA * POOL_WIDTH ** -0.5),
    }


def reference(x, c, ctx, c_ctx, mod_w, mod_b, ln_g, ln_b,
              ssm_w_in, ssm_conv_w, ssm_conv_b, ssm_dt_bias, ssm_a_log, ssm_d, ssm_norm_g, ssm_w_out,
              attn_w_in, attn_q_norm, attn_k_norm, attn_w_out,
              pool_w_in, pool_group_w, pool_scale, pool_w_out):
    n_tok = x.shape[1]
    cos, sin = _axial_rope(n_tok, x.dtype)
    sc = jax.nn.silu(c)
    scc = jax.nn.silu(c_ctx)
    for i in range(DEPTH):
        kind = i % N_MIXERS
        j = i // N_MIXERS
        compute_ctx = i < DEPTH - 1
        shift_x, scale_x, gate_x = jnp.split((sc @ mod_w[i] + mod_b[i])[:, None, :], 3, axis=-1)
        shift_c, scale_c, gate_c = jnp.split(scc @ mod_w[i] + mod_b[i], 3, axis=-1)
        hx = x * (1.0 + scale_x) + shift_x
        hc = ctx * (1.0 + scale_c) + shift_c
        if kind == 0:
            out_x, out_c = _mamba_mixer(hx, hc, ssm_w_in[j], ssm_conv_w[j], ssm_conv_b[j], ssm_dt_bias[j],
                                        ssm_a_log[j], ssm_d[j], ssm_norm_g[j], ssm_w_out[j], compute_ctx)
        elif kind == 1:
            out_x, out_c = _attn_mixer(hx, hc, attn_w_in[j], attn_q_norm[j], attn_k_norm[j], attn_w_out[j],
                                       cos, sin, compute_ctx)
        else:
            out_x, out_c = _pool_mixer(hx, hc, pool_w_in[j], pool_group_w[j], pool_scale[j], pool_w_out[j],
                                       compute_ctx)
        x = _layer_norm(DEEPNORM_ALPHA * x + gate_x * out_x, ln_g[i], ln_b[i])
        if compute_ctx:
            ctx = _layer_norm(DEEPNORM_ALPHA * ctx + gate_c * out_c, ln_g[i], ln_b[i])
    return x
```

```python
import functools

import jax
import jax.numpy as jnp
from jax import lax
from jax.experimental import pallas as pl
from jax.experimental.pallas import tpu as pltpu

F32 = jnp.float32
BF16 = jnp.bfloat16
HIGHEST = lax.Precision.HIGHEST

DEPTH = 4
N_MIXERS = 3
GRID_W = 64
ROPE_THETA = 10000.0

SSM_HEAD_DIM = 64
SSM_STATE = 128
SSM_GROUPS = 8
SSM_CONV = 5
SSM_CHUNK = 256

ATTN_HEADS = 16
ATTN_KV_HEADS = 4
ATTN_HEAD_DIM = 64

POOL_WINDOWS = (2, 4, 8, 16)

DEEPNORM_ALPHA = (2 * DEPTH) ** 0.25
LN_EPS = 1e-5
RMS_EPS = 1e-6

ROW_TILE = 256
HALO = 8
LANES = 128
COL_CHUNK = 512


def _silu(v):
    return v * jax.nn.sigmoid(v)


def _layer_norm(r, g, b):
    mu = jnp.mean(r, axis=-1, keepdims=True)
    d = r - mu
    var = jnp.mean(d * d, axis=-1, keepdims=True)
    return d * lax.rsqrt(var + LN_EPS) * g + b


def _resident(shape):
    nd = len(shape)
    return pl.BlockSpec(shape, lambda *_: (0,) * nd, pipeline_mode=pl.Buffered(1))


def _mod_spec(d_model, nct):
    return pl.BlockSpec((1, 8, d_model), lambda i: (jnp.where(i < nct, 1, 0), 0, 0))


def _halo_specs(d_model, tm, n_rows):
    per = tm // HALO
    last = n_rows // HALO - 1
    prev = pl.BlockSpec((HALO, d_model), lambda i: (jnp.maximum(i * per - 1, 0), 0))
    nxt = pl.BlockSpec((HALO, d_model), lambda i: (jnp.minimum((i + 1) * per, last), 0))
    return prev, nxt


def _halo_row_mask(i, tm, nct, ntiles):
    prev_ok = jnp.logical_and(i != 0, i != nct)
    next_ok = jnp.logical_and(i != nct - 1, i != ntiles - 1)
    lo = jnp.where(prev_ok, 0, HALO)
    hi = jnp.where(next_ok, tm + 2 * HALO, tm + HALO)
    rows = lax.broadcasted_iota(jnp.int32, (tm + 2 * HALO, 1), 0)
    return jnp.logical_and(rows >= lo, rows < hi).astype(F32)


def _mod_kernel(c_ref, w_ref, b_ref, o_ref):
    s = _silu(c_ref[...])
    o_ref[0] = jnp.dot(s, w_ref[0], precision=HIGHEST, preferred_element_type=F32) + b_ref[0]


def _mod_vectors(c, c_ctx, mod_w, mod_b):
    depth, d_model, d3 = mod_w.shape
    cpad = jnp.zeros((8, d_model), F32).at[0].set(c[0]).at[1].set(c_ctx)
    out = pl.pallas_call(
        _mod_kernel,
        grid=(depth, d3 // d_model),
        in_specs=[
            pl.BlockSpec((8, d_model), lambda i, j: (0, 0)),
            pl.BlockSpec((1, d_model, d_model), lambda i, j: (i, 0, j)),
            pl.BlockSpec((1, 1, d_model), lambda i, j: (i, 0, j)),
        ],
        out_specs=pl.BlockSpec((1, 8, d_model), lambda i, j: (i, 0, j)),
        out_shape=jax.ShapeDtypeStruct((depth, 8, d3), F32),
        name="mod_vectors",
    )(cpad, mod_w, mod_b.reshape(depth, 1, d3))
    return out


def _mamba_inproj_kernel(xp_ref, xm_ref, xn_ref, mod_ref, wz_ref, wxbc_ref, wdt_ref, cw_ref, cb_ref,
                         dtb_ref, z_ref, xbc_ref, dt_ref, dtT_ref, ext_scr, *, tm, nct, ntiles):
    i = pl.program_id(0)
    m = mod_ref[0]
    shift, scale = m[0:1], m[1:2]

    def modulate(v):
        return v * (1.0 + scale) + shift

    hx_m = modulate(xm_ref[...])
    hx_ext = jnp.concatenate([modulate(xp_ref[...]), hx_m, modulate(xn_ref[...])], axis=0).astype(BF16)
    hx_mb = hx_m.astype(BF16)
    maskf = _halo_row_mask(i, tm, nct, ntiles)
    pad = SSM_CONV // 2

    n_xbc = wxbc_ref.shape[1]
    for c in range(n_xbc // COL_CHUNK):
        cs = slice(c * COL_CHUNK, (c + 1) * COL_CHUNK)
        ext_scr[...] = jnp.dot(hx_ext, wxbc_ref[:, cs], preferred_element_type=F32) * maskf
        acc = cb_ref[:, cs] + cw_ref[0:1, cs] * ext_scr[pl.ds(HALO - pad, tm), :]
        for k in range(1, SSM_CONV):
            acc = acc + cw_ref[k:k + 1, cs] * ext_scr[pl.ds(HALO - pad + k, tm), :]
        xbc_ref[:, cs] = _silu(acc).astype(BF16)

    n_z = wz_ref.shape[1]
    for c in range(n_z // COL_CHUNK):
        cs = slice(c * COL_CHUNK, (c + 1) * COL_CHUNK)
        z_ref[:, cs] = jnp.dot(hx_mb, wz_ref[:, cs], preferred_element_type=F32).astype(BF16)

    dt_raw = jnp.dot(hx_m, wdt_ref[...], precision=HIGHEST, preferred_element_type=F32) + dtb_ref[...]
    dt = jnp.maximum(dt_raw, 0.0) + jnp.log1p(jnp.exp(-jnp.abs(dt_raw)))
    dt_ref[...] = dt
    dtT_ref[...] = dt.T


def _mamba_inproj(X, mod2, w_in, conv_w, conv_b, dt_bias, nct_rows):
    n_rows, d_model = X.shape
    tm = ROW_TILE
    ntiles, nct = n_rows // tm, nct_rows // tm
    d_inner = w_in.shape[1] - conv_w.shape[1] - 2 * (dt_bias.shape[-1])
    n_xbc = conv_w.shape[1]
    n_dt = 2 * dt_bias.shape[-1]
    wz = w_in[:, :d_inner].astype(BF16)
    wxbc = w_in[:, d_inner:d_inner + n_xbc].astype(BF16)
    wdt = jnp.zeros((d_model, LANES), F32).at[:, :n_dt].set(w_in[:, d_inner + n_xbc:])
    dtb = jnp.zeros((1, LANES), F32).at[0, :n_dt].set(dt_bias.reshape(-1))
    cw = jnp.zeros((8, n_xbc), F32).at[:SSM_CONV].set(conv_w)
    prev, nxt = _halo_specs(d_model, tm, n_rows)
    kern = functools.partial(_mamba_inproj_kernel, tm=tm, nct=nct, ntiles=ntiles)
    return pl.pallas_call(
        kern,
        grid=(ntiles,),
        in_specs=[
            prev,
            pl.BlockSpec((tm, d_model), lambda i: (i, 0)),
            nxt,
            _mod_spec(d_model, nct),
            _resident(wz.shape), _resident(wxbc.shape), _resident(wdt.shape),
            _resident(cw.shape), _resident((1, n_xbc)), _resident(dtb.shape),
        ],
        out_specs=[
            pl.BlockSpec((tm, d_inner), lambda i: (i, 0)),
            pl.BlockSpec((tm, n_xbc), lambda i: (i, 0)),
            pl.BlockSpec((tm, LANES), lambda i: (i, 0)),
            pl.BlockSpec((LANES, tm), lambda i: (0, i)),
        ],
        out_shape=[
            jax.ShapeDtypeStruct((n_rows, d_inner), BF16),
            jax.ShapeDtypeStruct((n_rows, n_xbc), BF16),
            jax.ShapeDtypeStruct((n_rows, LANES), F32),
            jax.ShapeDtypeStruct((LANES, n_rows), F32),
        ],
        scratch_shapes=[pltpu.VMEM((tm + 2 * HALO, COL_CHUNK), F32)],
        name="mamba_inproj",
    )(X, X, X, mod2, wz, wxbc, wdt, cw, conv_b.reshape(1, n_xbc), dtb)


def _ssd_direction(d, xs_ref, b_ref, c_ref, dt_ref, dtT_ref, arow_ref, acol_ref, r2_ref, y_ref,
                   state_scr, lower, upper, lane_head, *, L, n_heads):
    n_groups = SSM_GROUPS
    hpg = n_heads // n_groups
    gw = hpg * SSM_HEAD_DIM
    ns = SSM_STATE
    dt = dt_ref[...]
    dA = dt * arow_ref[0:1, :]
    dAT = dtT_ref[...] * acol_ref[...]
    lowf, upf = lower.astype(F32), upper.astype(F32)
    if d == 0:
        cum = jnp.dot(lowf, dA, precision=HIGHEST, preferred_element_type=F32)
        cumT = jnp.dot(dAT, upf, precision=HIGHEST, preferred_element_type=F32)
        tot = cum[L - 1:L]
        mask = lower
    else:
        cum = jnp.dot(upf, dA, precision=HIGHEST, preferred_element_type=F32)
        cumT = jnp.dot(dAT, lowf, precision=HIGHEST, preferred_element_type=F32)
        tot = cum[0:1]
        mask = upper
    e_in = jnp.exp(cum)
    tail = jnp.exp(tot - cum)
    st = jnp.concatenate([dt, dt * tail, e_in], axis=0)
    hi = st.astype(BF16)
    lo = (st - hi.astype(F32)).astype(BF16)
    lhs = jnp.concatenate([hi, lo], axis=1)
    neg_inf = jnp.float32(-jnp.inf)

    for g in range(n_groups):
        gs = slice(g * gw, (g + 1) * gw)
        rep = jnp.dot(lhs, r2_ref[:, gs], preferred_element_type=F32)
        xs_g = xs_ref[:, gs].astype(F32)
        xdt = xs_g * rep[0:L]
        xdt_tail = (xs_g * rep[L:2 * L]).astype(BF16)
        e_rep = rep[2 * L:3 * L]
        bg = b_ref[:, g * ns:(g + 1) * ns]
        cg = c_ref[:, g * ns:(g + 1) * ns]
        cb = lax.dot_general(cg, bg, (((1,), (1,)), ((), ())), preferred_element_type=F32)
        y = None
        for k in range(hpg):
            lane = d * n_heads + g * hpg + k
            seg = cum[:, lane:lane + 1] - cumT[lane:lane + 1, :]
            dec = jnp.exp(jnp.where(mask, seg, neg_inf))
            sc = (cb * dec).astype(BF16)
            xk = jnp.where(lane_head == k, xdt, 0.0).astype(BF16)
            yk = jnp.dot(sc, xk, preferred_element_type=F32)
            y = yk if y is None else y + yk
        h_t = state_scr[d, g]
        y_off = jnp.dot(cg, h_t.astype(BF16), preferred_element_type=F32) * e_rep
        e_tot = e_rep[L - 1:L] if d == 0 else e_rep[0:1]
        upd = lax.dot_general(bg, xdt_tail, (((0,), (0,)), ((), ())), preferred_element_type=F32)
        state_scr[d, g] = h_t * e_tot + upd
        y_ref[:, gs] = (y + y_off).astype(BF16)


def _ssd_kernel(xsf, bf, cf, dtf, dtTf, xsb, bb, cb_, dtb, dtTb, arow_ref, acol_ref, r2f_ref, r2b_ref,
                yf_ref, yb_ref, state_scr, *, L, n_heads):
    @pl.when(pl.program_id(0) == 0)
    def _():
        state_scr[...] = jnp.zeros_like(state_scr)

    r = lax.broadcasted_iota(jnp.int32, (L, L), 0)
    c = lax.broadcasted_iota(jnp.int32, (L, L), 1)
    lower, upper = c <= r, r <= c
    gw = (n_heads // SSM_GROUPS) * SSM_HEAD_DIM
    lane_head = lax.broadcasted_iota(jnp.int32, (1, gw), 1) // SSM_HEAD_DIM
    common = dict(L=L, n_heads=n_heads)
    _ssd_direction(0, xsf, bf, cf, dtf, dtTf, arow_ref, acol_ref, r2f_ref, yf_ref, state_scr,
                   lower, upper, lane_head, **common)
    _ssd_direction(1, xsb, bb, cb_, dtb, dtTb, arow_ref, acol_ref, r2b_ref, yb_ref, state_scr,
                   lower, upper, lane_head, **common)


def _ssd_scan(xbc, dt, dtT, a_log, d_inner):
    n_rows = xbc.shape[0]
    L = SSM_CHUNK
    n = n_rows // L
    n_heads = a_log.shape[-1]
    n_bc = SSM_GROUPS * SSM_STATE
    a = -jnp.exp(a_log.astype(F32)).reshape(-1)
    a_pad = jnp.zeros((LANES,), F32).at[:2 * n_heads].set(a)
    arow = jnp.zeros((8, LANES), F32).at[0].set(a_pad)
    acol = jnp.broadcast_to(a_pad[:, None], (LANES, L))
    head_of_col = jnp.arange(d_inner) // SSM_HEAD_DIM
    r2 = []
    for d in range(2):
        sel = (jnp.arange(LANES)[:, None] == (d * n_heads + head_of_col)[None, :]).astype(BF16)
        r2.append(jnp.concatenate([sel, sel], axis=0))
    fwd = lambda i: i
    bwd = lambda i: jnp.where(i == 0, 0, n - i)
    assert n_rows // L >= 2 and L == ROW_TILE

    def specs(cm):
        return [
            pl.BlockSpec((L, d_inner), lambda i: (cm(i), 0)),
            pl.BlockSpec((L, n_bc), lambda i: (cm(i), d_inner // n_bc)),
            pl.BlockSpec((L, n_bc), lambda i: (cm(i), d_inner // n_bc + 1)),
            pl.BlockSpec((L, LANES), lambda i: (cm(i), 0)),
            pl.BlockSpec((LANES, L), lambda i: (0, cm(i))),
        ]

    kern = functools.partial(_ssd_kernel, L=L, n_heads=n_heads)
    return pl.pallas_call(
        kern,
        grid=(n,),
        in_specs=specs(fwd) + specs(bwd) + [
            _resident(arow.shape), _resident(acol.shape), _resident(r2[0].shape), _resident(r2[1].shape)],
        out_specs=[
            pl.BlockSpec((L, d_inner), lambda i: (fwd(i), 0)),
            pl.BlockSpec((L, d_inner), lambda i: (bwd(i), 0)),
        ],
        out_shape=[jax.ShapeDtypeStruct((n_rows, d_inner), BF16)] * 2,
        scratch_shapes=[pltpu.VMEM((2, SSM_GROUPS, SSM_STATE, d_inner // SSM_GROUPS), F32)],
        compiler_params=pltpu.CompilerParams(dimension_semantics=("arbitrary",)),
        name="ssd_scan",
    )(xbc, xbc, xbc, dt, dtT, xbc, xbc, xbc, dt, dtT, arow, acol, r2[0], r2[1])


def _mamba_finish_kernel(yf_ref, yb_ref, xs_ref, z_ref, x_ref, mod_ref, drep_ref, ng_ref, wout_ref,
                         lng_ref, lnb_ref, o_ref):
    y = yf_ref[...].astype(F32) + yb_ref[...].astype(F32) + xs_ref[...].astype(F32) * drep_ref[...]
    t = y * _silu(z_ref[...].astype(F32))
    t = t * lax.rsqrt(jnp.mean(t * t, axis=-1, keepdims=True) + RMS_EPS) * ng_ref[...]
    out = jnp.dot(t.astype(BF16), wout_ref[...], preferred_element_type=F32)
    gate = mod_ref[0][2:3]
    o_ref[...] = _layer_norm(DEEPNORM_ALPHA * x_ref[...] + gate * out, lng_ref[...], lnb_ref[...])


def _mamba_finish(yf, yb, xbc, z, X, mod2, d_skip, norm_g, w_out, ln_g, ln_b, nct_rows):
    n_rows, d_model = X.shape
    d_inner = z.shape[1]
    tm = ROW_TILE
    ntiles, nct = n_rows // tm, nct_rows // tm
    drep = jnp.repeat(d_skip.astype(F32), SSM_HEAD_DIM).reshape(1, d_inner)
    row = lambda w: pl.BlockSpec((tm, w), lambda i: (i, 0))
    return pl.pallas_call(
        _mamba_finish_kernel,
        grid=(ntiles,),
        in_specs=[row(d_inner), row(d_inner), row(d_inner), row(d_inner), row(d_model),
                  _mod_spec(d_model, nct), _resident((1, d_inner)), _resident((1, d_inner)),
                  _resident(w_out.shape), _resident((1, d_model)), _resident((1, d_model))],
        out_specs=row(d_model),
        out_shape=jax.ShapeDtypeStruct((n_rows, d_model), F32),
        name="mamba_finish",
    )(yf, yb, xbc, z, X, mod2, drep, norm_g.reshape(1, d_inner), w_out.astype(BF16),
      ln_g.reshape(1, d_model), ln_b.reshape(1, d_model))


def _mamba_layer(X, mod2, w_in, conv_w, conv_b, dt_bias, a_log, d_skip, norm_g, w_out, ln_g, ln_b, nct_rows):
    z, xbc, dt, dtT = _mamba_inproj(X, mod2, w_in, conv_w, conv_b, dt_bias, nct_rows)
    yf, yb = _ssd_scan(xbc, dt, dtT, a_log, z.shape[1])
    return _mamba_finish(yf, yb, xbc, z, X, mod2, d_skip, norm_g, w_out, ln_g, ln_b, nct_rows)


def _attn_inproj_kernel(x_ref, mod_ref, w_ref, gm_ref, qn_ref, kn_ref, cos_ref, sin_ref,
                        qT_ref, k_ref, vT_ref, g_ref, *, wq, wkv):
    m = mod_ref[0]
    hx = (x_ref[...] * (1.0 + m[1:2]) + m[0:1]).astype(BF16)
    hd = ATTN_HEAD_DIM

    def norm_rope(t, gmean, gain):
        width = t.shape[1]
        ms = jnp.dot((t * t).astype(BF16), gmean, preferred_element_type=F32)
        tn = t * lax.rsqrt(ms + RMS_EPS) * gain
        reps = width // LANES
        cs = jnp.concatenate([cos_ref[...]] * reps, axis=1)
        sn = jnp.concatenate([sin_ref[...]] * reps, axis=1)
        lane = lax.broadcasted_iota(jnp.int32, (1, width), 1)
        first_half = (lane % hd) < hd // 2
        swapped = jnp.where(first_half, pltpu.roll(tn, width - hd // 2, 1), pltpu.roll(tn, hd // 2, 1))
        return tn * cs + swapped * sn

    q = jnp.dot(hx, w_ref[:, 0:wq], preferred_element_type=F32)
    qr = norm_rope(q, gm_ref[...], qn_ref[...]) * (hd ** -0.5)
    qT_ref[...] = qr.T.astype(BF16)
    k = jnp.dot(hx, w_ref[:, wq:wq + wkv], preferred_element_type=F32)
    kr = norm_rope(k, gm_ref[0:wkv, 0:wkv], kn_ref[...])
    for j in range(wkv // hd):
        k_ref[j] = kr[:, j * hd:(j + 1) * hd].astype(BF16)
    v = jnp.dot(hx, w_ref[:, wq + wkv:wq + 2 * wkv], preferred_element_type=F32)
    vT_ref[...] = v.T.astype(BF16)
    g_ref[...] = jnp.dot(hx, w_ref[:, wq + 2 * wkv:], preferred_element_type=F32).astype(BF16)


def _rope_tables(n_tok, n_ctx):
    t = jnp.arange(n_tok)
    row_ids = (t // GRID_W).astype(F32)
    col_ids = (t % GRID_W).astype(F32)
    half = ATTN_HEAD_DIM // 2
    inv = ROPE_THETA ** (-jnp.arange(0, half, 2, dtype=F32) / half)
    ang = jnp.concatenate([row_ids[:, None] * inv, col_ids[:, None] * inv], axis=-1)
    c, s = jnp.cos(ang), jnp.sin(ang)
    cos_h = jnp.concatenate([c, c], axis=-1)
    sin_h = jnp.concatenate([-s, s], axis=-1)
    reps = LANES // ATTN_HEAD_DIM
    cos2 = jnp.concatenate([jnp.ones((n_ctx, LANES), F32), jnp.tile(cos_h, (1, reps))], axis=0)
    sin2 = jnp.concatenate([jnp.zeros((n_ctx, LANES), F32), jnp.tile(sin_h, (1, reps))], axis=0)
    return cos2, sin2


def _attn_inproj(X, mod2, w_in, q_norm, k_norm, cos2, sin2, nct_rows):
    n_rows, d_model = X.shape
    tm = ROW_TILE
    ntiles, nct = n_rows // tm, nct_rows // tm
    hd = ATTN_HEAD_DIM
    wq, wkv = ATTN_HEADS * hd, ATTN_KV_HEADS * hd
    idx = jnp.arange(wq) // hd
    gmean = (idx[:, None] == idx[None, :]).astype(BF16) * (1.0 / hd)
    qn = jnp.tile(q_norm.astype(F32), ATTN_HEADS).reshape(1, wq)
    kn = jnp.tile(k_norm.astype(F32), ATTN_KV_HEADS).reshape(1, wkv)
    w = w_in.astype(BF16)
    kern = functools.partial(_attn_inproj_kernel, wq=wq, wkv=wkv)
    return pl.pallas_call(
        kern,
        grid=(ntiles,),
        in_specs=[
            pl.BlockSpec((tm, d_model), lambda i: (i, 0)),
            _mod_spec(d_model, nct),
            _resident(w.shape), _resident(gmean.shape), _resident(qn.shape), _resident(kn.shape),
            pl.BlockSpec((tm, LANES), lambda i: (i, 0)),
            pl.BlockSpec((tm, LANES), lambda i: (i, 0)),
        ],
        out_specs=[
            pl.BlockSpec((wq, tm), lambda i: (0, i)),
            pl.BlockSpec((ATTN_KV_HEADS, tm, hd), lambda i: (0, i, 0)),
            pl.BlockSpec((wkv, tm), lambda i: (0, i)),
            pl.BlockSpec((tm, wq), lambda i: (i, 0)),
        ],
        out_shape=[
            jax.ShapeDtypeStruct((wq, n_rows), BF16),
            jax.ShapeDtypeStruct((ATTN_KV_HEADS, n_rows, hd), BF16),
            jax.ShapeDtypeStruct((wkv, n_rows), BF16),
            jax.ShapeDtypeStruct((n_rows, wq), BF16),
        ],
        name="attn_inproj",
    )(X, mod2, w, gmean, qn, kn, cos2, sin2)


def _flash_kernel(qT_ref, k_ref, vT_ref, o_ref, m_scr, l_scr, acc_scr, *, bk, nct_tiles, nct_chunks, n_chunks):
    qi = pl.program_id(1)
    hd = ATTN_HEAD_DIM
    n_rep = ATTN_HEADS // ATTN_KV_HEADS
    m_scr[...] = jnp.full_like(m_scr, -jnp.inf)
    l_scr[...] = jnp.zeros_like(l_scr)
    acc_scr[...] = jnp.zeros_like(acc_scr)
    nch = jnp.where(qi < nct_tiles, nct_chunks, n_chunks)

    def body(j, carry):
        off = pl.multiple_of(j * bk, bk)
        kb = k_ref[0, pl.ds(off, bk), :]
        vb = vT_ref[:, pl.ds(off, bk)]
        for g in range(n_rep):
            qg = qT_ref[g * hd:(g + 1) * hd, :]
            s = jnp.dot(kb, qg, preferred_element_type=F32)
            m_old = m_scr[g]
            m_new = jnp.maximum(m_old, jnp.max(s, axis=0, keepdims=True))
            alpha = jnp.exp(m_old - m_new)
            p = jnp.exp(s - m_new)
            l_scr[g] = alpha * l_scr[g] + jnp.sum(p, axis=0, keepdims=True)
            acc_scr[g] = alpha * acc_scr[g] + jnp.dot(vb, p.astype(BF16), preferred_element_type=F32)
            m_scr[g] = m_new
        return carry

    lax.fori_loop(0, nch, body, 0)
    o_t = jnp.concatenate([acc_scr[g] / l_scr[g] for g in range(n_rep)], axis=0)
    o_ref[...] = o_t.T.astype(BF16)


def _flash_attention(qT, k, vT, nct_rows):
    wq, n_rows = qT.shape
    hd = ATTN_HEAD_DIM
    n_rep = ATTN_HEADS // ATTN_KV_HEADS
    bq = bk = ROW_TILE
    kern = functools.partial(_flash_kernel, bk=bk, nct_tiles=nct_rows // bq, nct_chunks=nct_rows // bk,
                             n_chunks=n_rows // bk)
    return pl.pallas_call(
        kern,
        grid=(ATTN_KV_HEADS, n_rows // bq),
        in_specs=[
            pl.BlockSpec((n_rep * hd, bq), lambda h, i: (h, i)),
            pl.BlockSpec((1, n_rows, hd), lambda h, i: (h, 0, 0)),
            pl.BlockSpec((hd, n_rows), lambda h, i: (h, 0)),
        ],
        out_specs=pl.BlockSpec((bq, n_rep * hd), lambda h, i: (i, h)),
        out_shape=jax.ShapeDtypeStruct((n_rows, wq), BF16),
        scratch_shapes=[
            pltpu.VMEM((n_rep, 1, bq), F32),
            pltpu.VMEM((n_rep, 1, bq), F32),
            pltpu.VMEM((n_rep, hd, bq), F32),
        ],
        name="flash_attention",
    )(qT, k, vT)


def _attn_out_kernel(o_ref, g_ref, x_ref, mod_ref, wout_ref, lng_ref, lnb_ref, out_ref):
    t = o_ref[...].astype(F32) * _silu(g_ref[...].astype(F32))
    out = jnp.dot(t.astype(BF16), wout_ref[...], preferred_element_type=F32)
    gate = mod_ref[0][2:3]
    out_ref[...] = _layer_norm(DEEPNORM_ALPHA * x_ref[...] + gate * out, lng_ref[...], lnb_ref[...])


def _attn_out(o, g, X, mod2, w_out, ln_g, ln_b, nct_rows):
    n_rows, d_model = X.shape
    wq = o.shape[1]
    tm = ROW_TILE
    ntiles, nct = n_rows // tm, nct_rows // tm
    row = lambda w: pl.BlockSpec((tm, w), lambda i: (i, 0))
    return pl.pallas_call(
        _attn_out_kernel,
        grid=(ntiles,),
        in_specs=[row(wq), row(wq), row(d_model), _mod_spec(d_model, nct), _resident(w_out.shape),
                  _resident((1, d_model)), _resident((1, d_model))],
        out_specs=row(d_model),
        out_shape=jax.ShapeDtypeStruct((n_rows, d_model), F32),
        name="attn_out",
    )(o, g, X, mod2, w_out.astype(BF16), ln_g.reshape(1, d_model), ln_b.reshape(1, d_model))


def _attn_layer(X, mod2, w_in, q_norm, k_norm, w_out, ln_g, ln_b, cos2, sin2, nct_rows):
    qT, k, vT, g = _attn_inproj(X, mod2, w_in, q_norm, k_norm, cos2, sin2, nct_rows)
    o = _flash_attention(qT, k, vT, nct_rows)
    return _attn_out(o, g, X, mod2, w_out, ln_g, ln_b, nct_rows)


def _pool_kernel(xp_ref, xm_ref, xn_ref, mod_ref, win_ref, gw_ref, ls_ref, wout_ref, lng_ref, lnb_ref,
                 o_ref, ext_scr, t_scr, *, tm, nct, ntiles, n_ctx, n_tok):
    i = pl.program_id(0)
    m = mod_ref[0]
    shift, scale, gate = m[0:1], m[1:2], m[2:3]

    def modulate(v):
        return v * (1.0 + scale) + shift

    x_m = xm_ref[...]
    hx_m = modulate(x_m)
    hx_ext = jnp.concatenate([modulate(xp_ref[...]), hx_m, modulate(xn_ref[...])], axis=0).astype(BF16)
    hx_mb = hx_m.astype(BF16)
    maskf = _halo_row_mask(i, tm, nct, ntiles)

    is_ctx = i < nct
    t0 = (i - jnp.where(is_ctx, 0, nct)) * tm
    n_seq = jnp.where(is_ctx, n_ctx, n_tok)
    pos = t0 + lax.broadcasted_iota(jnp.int32, (tm, 1), 0)

    n_groups = len(POOL_WINDOWS)
    gd = gw_ref.shape[1]
    width = n_groups * gd
    for gi, window in enumerate(POOL_WINDOWS):
        cs = slice(gi * gd, (gi + 1) * gd)
        ext_scr[...] = jnp.dot(hx_ext, win_ref[:, cs], preferred_element_type=F32) * maskf
        lo_off = window // 2
        hi_off = window - 1 - lo_off
        win_sum = ext_scr[pl.ds(HALO - lo_off, tm), :]
        for o in range(-lo_off + 1, hi_off + 1):
            win_sum = win_sum + ext_scr[pl.ds(HALO + o, tm), :]
        cnt = (jnp.minimum(pos + hi_off + 1, n_seq) - jnp.maximum(pos - lo_off, 0)).astype(F32)
        pooled = win_sum / cnt - ext_scr[pl.ds(HALO, tm), :]
        mixed = jnp.dot(pooled.astype(BF16), gw_ref[gi], preferred_element_type=F32)
        zc = jnp.dot(hx_mb, win_ref[:, width + gi * gd:width + (gi + 1) * gd], preferred_element_type=F32)
        t_scr[:, cs] = ((mixed * ls_ref[:, cs]) * _silu(zc)).astype(BF16)
    out = jnp.dot(t_scr[...], wout_ref[...], preferred_element_type=F32)
    o_ref[...] = _layer_norm(DEEPNORM_ALPHA * x_m + gate * out, lng_ref[...], lnb_ref[...])


def _pool_layer(X, mod2, w_in, group_w, layer_scale, w_out, ln_g, ln_b, nct_rows):
    n_rows, d_model = X.shape
    tm = ROW_TILE
    ntiles, nct = n_rows // tm, nct_rows // tm
    width = w_out.shape[0]
    gd = group_w.shape[1]
    assert max(POOL_WINDOWS) // 2 <= HALO
    prev, nxt = _halo_specs(d_model, tm, n_rows)
    kern = functools.partial(_pool_kernel, tm=tm, nct=nct, ntiles=ntiles, n_ctx=nct_rows,
                             n_tok=n_rows - nct_rows)
    return pl.pallas_call(
        kern,
        grid=(ntiles,),
        in_specs=[
            prev,
            pl.BlockSpec((tm, d_model), lambda i: (i, 0)),
            nxt,
            _mod_spec(d_model, nct),
            _resident(w_in.shape), _resident(group_w.shape), _resident((1, width)), _resident(w_out.shape),
            _resident((1, d_model)), _resident((1, d_model)),
        ],
        out_specs=pl.BlockSpec((tm, d_model), lambda i: (i, 0)),
        out_shape=jax.ShapeDtypeStruct((n_rows, d_model), F32),
        scratch_shapes=[pltpu.VMEM((tm + 2 * HALO, gd), F32), pltpu.VMEM((tm, width), BF16)],
        name="pool_mixer",
    )(X, X, X, mod2, w_in.astype(BF16), group_w.astype(BF16), layer_scale.reshape(1, width),
      w_out.astype(BF16), ln_g.reshape(1, d_model), ln_b.reshape(1, d_model))


def kernel(x, c, ctx, c_ctx, mod_w, mod_b, ln_g, ln_b, ssm_w_in, ssm_conv_w, ssm_conv_b, ssm_dt_bias,
           ssm_a_log, ssm_d, ssm_norm_g, ssm_w_out, attn_w_in, attn_q_norm, attn_k_norm, attn_w_out,
           pool_w_in, pool_group_w, pool_scale, pool_w_out):
    bsz, n_tok, d_model = x.shape
    n_ctx = ctx.shape[1]
    assert bsz == 1 and n_ctx % ROW_TILE == 0 and n_tok % ROW_TILE == 0
    X = jnp.concatenate([ctx[0], x[0]], axis=0)
    mods = _mod_vectors(c, c_ctx, mod_w, mod_b)
    cos2, sin2 = _rope_tables(n_tok, n_ctx)
    for i in range(DEPTH):
        kind, j = i % N_MIXERS, i // N_MIXERS
        m3 = mods[i, 0:2].reshape(2, 3, d_model)
        mod2 = jnp.zeros((2, 8, d_model), F32).at[:, 0:3].set(m3)
        if kind == 0:
            X = _mamba_layer(X, mod2, ssm_w_in[j], ssm_conv_w[j], ssm_conv_b[j], ssm_dt_bias[j],
                             ssm_a_log[j], ssm_d[j], ssm_norm_g[j], ssm_w_out[j], ln_g[i], ln_b[i], n_ctx)
        elif kind == 1:
            X = _attn_layer(X, mod2, attn_w_in[j], attn_q_norm[j], attn_k_norm[j], attn_w_out[j],
                            ln_g[i], ln_b[i], cos2, sin2, n_ctx)
        else:
            X = _pool_layer(X, mod2, pool_w_in[j], pool_group_w[j], pool_scale[j], pool_w_out[j],
                            ln_g[i], ln_b[i], n_ctx)
    return X[n_ctx:][None]
```

```python
import functools

import jax
import jax.numpy as jnp
from jax import lax
from jax.experimental import pallas as pl
from jax.experimental.pallas import tpu as pltpu

F32 = jnp.float32
BF16 = jnp.bfloat16
HIGHEST = lax.Precision.HIGHEST

DEPTH = 4
N_MIXERS = 3
GRID_W = 64
ROPE_THETA = 10000.0

SSM_HEAD_DIM = 64
SSM_STATE = 128
SSM_GROUPS = 8
SSM_CONV = 5
SSM_CHUNK = 256

ATTN_HEADS = 16
ATTN_KV_HEADS = 4
ATTN_HEAD_DIM = 64
V_ROWS = ATTN_HEAD_DIM + 16
LOG2_E = 1.4426950408889634

POOL_WINDOWS = (2, 4, 8, 16)

DEEPNORM_ALPHA = (2 * DEPTH) ** 0.25
LN_EPS = 1e-5
RMS_EPS = 1e-6

ROW_TILE = 256
HALO = 8
LANES = 128
COL_CHUNK = 512


def _silu(v):
    return v * jax.nn.sigmoid(v)


def _layer_norm(r, g, b):
    mu = jnp.mean(r, axis=-1, keepdims=True)
    d = r - mu
    var = jnp.mean(d * d, axis=-1, keepdims=True)
    return d * lax.rsqrt(var + LN_EPS) * g + b


def _resident(shape):
    nd = len(shape)
    return pl.BlockSpec(shape, lambda *_: (0,) * nd, pipeline_mode=pl.Buffered(1))


def _mod_spec(d_model, nct):
    return pl.BlockSpec((1, 8, d_model), lambda i: (jnp.where(i < nct, 1, 0), 0, 0))


def _halo_specs(d_model, tm, n_rows):
    per = tm // HALO
    last = n_rows // HALO - 1
    prev = pl.BlockSpec((HALO, d_model), lambda i: (jnp.maximum(i * per - 1, 0), 0))
    nxt = pl.BlockSpec((HALO, d_model), lambda i: (jnp.minimum((i + 1) * per, last), 0))
    return prev, nxt


def _halo_row_mask(i, tm, nct, ntiles):
    prev_ok = jnp.logical_and(i != 0, i != nct)
    next_ok = jnp.logical_and(i != nct - 1, i != ntiles - 1)
    lo = jnp.where(prev_ok, 0, HALO)
    hi = jnp.where(next_ok, tm + 2 * HALO, tm + HALO)
    rows = lax.broadcasted_iota(jnp.int32, (tm + 2 * HALO, 1), 0)
    return jnp.logical_and(rows >= lo, rows < hi).astype(F32)


def _mod_kernel(c_ref, w_ref, b_ref, o_ref):
    s = _silu(c_ref[...])
    o_ref[0] = jnp.dot(s, w_ref[0], precision=HIGHEST, preferred_element_type=F32) + b_ref[0]


def _mod_vectors(c, c_ctx, mod_w, mod_b):
    depth, d_model, d3 = mod_w.shape
    cpad = jnp.zeros((8, d_model), F32).at[0].set(c[0]).at[1].set(c_ctx)
    out = pl.pallas_call(
        _mod_kernel,
        grid=(depth, d3 // d_model),
        in_specs=[
            pl.BlockSpec((8, d_model), lambda i, j: (0, 0)),
            pl.BlockSpec((1, d_model, d_model), lambda i, j: (i, 0, j)),
            pl.BlockSpec((1, 1, d_model), lambda i, j: (i, 0, j)),
        ],
        out_specs=pl.BlockSpec((1, 8, d_model), lambda i, j: (i, 0, j)),
        out_shape=jax.ShapeDtypeStruct((depth, 8, d3), F32),
        name="mod_vectors",
    )(cpad, mod_w, mod_b.reshape(depth, 1, d3))
    return out


def _mamba_inproj_kernel(xp_ref, xm_ref, xn_ref, mod_ref, wz_ref, wxbc_ref, wdt_ref, cw_ref, cb_ref,
                         dtb_ref, z_ref, xbc_ref, dt_ref, dtT_ref, ext_scr, *, tm, nct, ntiles):
    i = pl.program_id(0)
    m = mod_ref[0]
    shift, scale = m[0:1], m[1:2]

    def modulate(v):
        return v * (1.0 + scale) + shift

    hx_m = modulate(xm_ref[...])
    hx_ext = jnp.concatenate([modulate(xp_ref[...]), hx_m, modulate(xn_ref[...])], axis=0).astype(BF16)
    hx_mb = hx_m.astype(BF16)
    maskf = _halo_row_mask(i, tm, nct, ntiles)
    pad = SSM_CONV // 2

    n_xbc = wxbc_ref.shape[1]
    for c in range(n_xbc // COL_CHUNK):
        cs = slice(c * COL_CHUNK, (c + 1) * COL_CHUNK)
        ext_scr[...] = jnp.dot(hx_ext, wxbc_ref[:, cs], preferred_element_type=F32) * maskf
        acc = cb_ref[:, cs] + cw_ref[0:1, cs] * ext_scr[pl.ds(HALO - pad, tm), :]
        for k in range(1, SSM_CONV):
            acc = acc + cw_ref[k:k + 1, cs] * ext_scr[pl.ds(HALO - pad + k, tm), :]
        xbc_ref[:, cs] = _silu(acc).astype(BF16)

    n_z = wz_ref.shape[1]
    for c in range(n_z // COL_CHUNK):
        cs = slice(c * COL_CHUNK, (c + 1) * COL_CHUNK)
        z_ref[:, cs] = jnp.dot(hx_mb, wz_ref[:, cs], preferred_element_type=F32).astype(BF16)

    dt_raw = jnp.dot(hx_m, wdt_ref[...], precision=HIGHEST, preferred_element_type=F32) + dtb_ref[...]
    dt = jnp.maximum(dt_raw, 0.0) + jnp.log1p(jnp.exp(-jnp.abs(dt_raw)))
    dt_ref[...] = dt
    dtT_ref[...] = dt.T


def _mamba_inproj(X, mod2, w_in, conv_w, conv_b, dt_bias, nct_rows):
    n_rows, d_model = X.shape
    tm = ROW_TILE
    ntiles, nct = n_rows // tm, nct_rows // tm
    d_inner = w_in.shape[1] - conv_w.shape[1] - 2 * (dt_bias.shape[-1])
    n_xbc = conv_w.shape[1]
    n_dt = 2 * dt_bias.shape[-1]
    wz = w_in[:, :d_inner].astype(BF16)
    wxbc = w_in[:, d_inner:d_inner + n_xbc].astype(BF16)
    wdt = jnp.zeros((d_model, LANES), F32).at[:, :n_dt].set(w_in[:, d_inner + n_xbc:])
    dtb = jnp.zeros((1, LANES), F32).at[0, :n_dt].set(dt_bias.reshape(-1))
    cw = jnp.zeros((8, n_xbc), F32).at[:SSM_CONV].set(conv_w)
    prev, nxt = _halo_specs(d_model, tm, n_rows)
    kern = functools.partial(_mamba_inproj_kernel, tm=tm, nct=nct, ntiles=ntiles)
    return pl.pallas_call(
        kern,
        grid=(ntiles,),
        in_specs=[
            prev,
            pl.BlockSpec((tm, d_model), lambda i: (i, 0)),
            nxt,
            _mod_spec(d_model, nct),
            _resident(wz.shape), _resident(wxbc.shape), _resident(wdt.shape),
            _resident(cw.shape), _resident((1, n_xbc)), _resident(dtb.shape),
        ],
        out_specs=[
            pl.BlockSpec((tm, d_inner), lambda i: (i, 0)),
            pl.BlockSpec((tm, n_xbc), lambda i: (i, 0)),
            pl.BlockSpec((tm, LANES), lambda i: (i, 0)),
            pl.BlockSpec((LANES, tm), lambda i: (0, i)),
        ],
        out_shape=[
            jax.ShapeDtypeStruct((n_rows, d_inner), BF16),
            jax.ShapeDtypeStruct((n_rows, n_xbc), BF16),
            jax.ShapeDtypeStruct((n_rows, LANES), F32),
            jax.ShapeDtypeStruct((LANES, n_rows), F32),
        ],
        scratch_shapes=[pltpu.VMEM((tm + 2 * HALO, COL_CHUNK), F32)],
        name="mamba_inproj",
    )(X, X, X, mod2, wz, wxbc, wdt, cw, conv_b.reshape(1, n_xbc), dtb)


def _ssd_direction(d, xs_ref, b_ref, c_ref, dt_ref, dtT_ref, arow_ref, acol_ref, r2_ref, y_ref,
                   state_scr, lower, upper, lane_head, *, L, n_heads):
    n_groups = SSM_GROUPS
    hpg = n_heads // n_groups
    gw = hpg * SSM_HEAD_DIM
    ns = SSM_STATE
    dt = dt_ref[...]
    dA = dt * arow_ref[0:1, :]
    dAT = dtT_ref[...] * acol_ref[...]
    lowf, upf = lower.astype(F32), upper.astype(F32)
    if d == 0:
        cum = jnp.dot(lowf, dA, precision=HIGHEST, preferred_element_type=F32)
        cumT = jnp.dot(dAT, upf, precision=HIGHEST, preferred_element_type=F32)
        tot = cum[L - 1:L]
        mask = lower
    else:
        cum = jnp.dot(upf, dA, precision=HIGHEST, preferred_element_type=F32)
        cumT = jnp.dot(dAT, lowf, precision=HIGHEST, preferred_element_type=F32)
        tot = cum[0:1]
        mask = upper
    e_in = jnp.exp(cum)
    tail = jnp.exp(tot - cum)
    st = jnp.concatenate([dt, dt * tail, e_in], axis=0)
    hi = st.astype(BF16)
    lo = (st - hi.astype(F32)).astype(BF16)
    lhs = jnp.concatenate([hi, lo], axis=1)
    neg_inf = jnp.float32(-jnp.inf)

    for g in range(n_groups):
        gs = slice(g * gw, (g + 1) * gw)
        rep = jnp.dot(lhs, r2_ref[:, gs], preferred_element_type=F32)
        xs_g = xs_ref[:, gs].astype(F32)
        xdt = xs_g * rep[0:L]
        xdt_tail = (xs_g * rep[L:2 * L]).astype(BF16)
        e_rep = rep[2 * L:3 * L]
        bg = b_ref[:, g * ns:(g + 1) * ns]
        cg = c_ref[:, g * ns:(g + 1) * ns]
        cb = lax.dot_general(cg, bg, (((1,), (1,)), ((), ())), preferred_element_type=F32)
        y = None
        for k in range(hpg):
            lane = d * n_heads + g * hpg + k
            seg = cum[:, lane:lane + 1] - cumT[lane:lane + 1, :]
            dec = jnp.exp(jnp.where(mask, seg, neg_inf))
            sc = (cb * dec).astype(BF16)
            xk = jnp.where(lane_head == k, xdt, 0.0).astype(BF16)
            yk = jnp.dot(sc, xk, preferred_element_type=F32)
            y = yk if y is None else y + yk
        h_t = state_scr[d, g]
        y_off = jnp.dot(cg, h_t.astype(BF16), preferred_element_type=F32) * e_rep
        e_tot = e_rep[L - 1:L] if d == 0 else e_rep[0:1]
        upd = lax.dot_general(bg, xdt_tail, (((0,), (0,)), ((), ())), preferred_element_type=F32)
        state_scr[d, g] = h_t * e_tot + upd
        y_ref[:, gs] = (y + y_off).astype(BF16)


def _ssd_kernel(xsf, bf, cf, dtf, dtTf, xsb, bb, cb_, dtb, dtTb, arow_ref, acol_ref, r2f_ref, r2b_ref,
                yf_ref, yb_ref, state_scr, *, L, n_heads):
    @pl.when(pl.program_id(0) == 0)
    def _():
        state_scr[...] = jnp.zeros_like(state_scr)

    r = lax.broadcasted_iota(jnp.int32, (L, L), 0)
    c = lax.broadcasted_iota(jnp.int32, (L, L), 1)
    lower, upper = c <= r, r <= c
    gw = (n_heads // SSM_GROUPS) * SSM_HEAD_DIM
    lane_head = lax.broadcasted_iota(jnp.int32, (1, gw), 1) // SSM_HEAD_DIM
    common = dict(L=L, n_heads=n_heads)
    _ssd_direction(0, xsf, bf, cf, dtf, dtTf, arow_ref, acol_ref, r2f_ref, yf_ref, state_scr,
                   lower, upper, lane_head, **common)
    _ssd_direction(1, xsb, bb, cb_, dtb, dtTb, arow_ref, acol_ref, r2b_ref, yb_ref, state_scr,
                   lower, upper, lane_head, **common)


def _ssd_scan(xbc, dt, dtT, a_log, d_inner):
    n_rows = xbc.shape[0]
    L = SSM_CHUNK
    n = n_rows // L
    n_heads = a_log.shape[-1]
    n_bc = SSM_GROUPS * SSM_STATE
    a = -jnp.exp(a_log.astype(F32)).reshape(-1)
    a_pad = jnp.zeros((LANES,), F32).at[:2 * n_heads].set(a)
    arow = jnp.zeros((8, LANES), F32).at[0].set(a_pad)
    acol = jnp.broadcast_to(a_pad[:, None], (LANES, L))
    head_of_col = jnp.arange(d_inner) // SSM_HEAD_DIM
    r2 = []
    for d in range(2):
        sel = (jnp.arange(LANES)[:, None] == (d * n_heads + head_of_col)[None, :]).astype(BF16)
        r2.append(jnp.concatenate([sel, sel], axis=0))
    fwd = lambda i: i
    bwd = lambda i: jnp.where(i == 0, 0, n - i)
    assert n_rows // L >= 2 and L == ROW_TILE

    def specs(cm):
        return [
            pl.BlockSpec((L, d_inner), lambda i: (cm(i), 0)),
            pl.BlockSpec((L, n_bc), lambda i: (cm(i), d_inner // n_bc)),
            pl.BlockSpec((L, n_bc), lambda i: (cm(i), d_inner // n_bc + 1)),
            pl.BlockSpec((L, LANES), lambda i: (cm(i), 0)),
            pl.BlockSpec((LANES, L), lambda i: (0, cm(i))),
        ]

    kern = functools.partial(_ssd_kernel, L=L, n_heads=n_heads)
    return pl.pallas_call(
        kern,
        grid=(n,),
        in_specs=specs(fwd) + specs(bwd) + [
            _resident(arow.shape), _resident(acol.shape), _resident(r2[0].shape), _resident(r2[1].shape)],
        out_specs=[
            pl.BlockSpec((L, d_inner), lambda i: (fwd(i), 0)),
            pl.BlockSpec((L, d_inner), lambda i: (bwd(i), 0)),
        ],
        out_shape=[jax.ShapeDtypeStruct((n_rows, d_inner), BF16)] * 2,
        scratch_shapes=[pltpu.VMEM((2, SSM_GROUPS, SSM_STATE, d_inner // SSM_GROUPS), F32)],
        compiler_params=pltpu.CompilerParams(dimension_semantics=("arbitrary",)),
        name="ssd_scan",
    )(xbc, xbc, xbc, dt, dtT, xbc, xbc, xbc, dt, dtT, arow, acol, r2[0], r2[1])


def _mamba_finish_kernel(yf_ref, yb_ref, xs_ref, z_ref, x_ref, mod_ref, drep_ref, ng_ref, wout_ref,
                         lng_ref, lnb_ref, o_ref):
    y = yf_ref[...].astype(F32) + yb_ref[...].astype(F32) + xs_ref[...].astype(F32) * drep_ref[...]
    t = y * _silu(z_ref[...].astype(F32))
    t = t * lax.rsqrt(jnp.mean(t * t, axis=-1, keepdims=True) + RMS_EPS) * ng_ref[...]
    out = jnp.dot(t.astype(BF16), wout_ref[...], preferred_element_type=F32)
    gate = mod_ref[0][2:3]
    o_ref[...] = _layer_norm(DEEPNORM_ALPHA * x_ref[...] + gate * out, lng_ref[...], lnb_ref[...])


def _mamba_finish(yf, yb, xbc, z, X, mod2, d_skip, norm_g, w_out, ln_g, ln_b, nct_rows):
    n_rows, d_model = X.shape
    d_inner = z.shape[1]
    tm = ROW_TILE
    ntiles, nct = n_rows // tm, nct_rows // tm
    drep = jnp.repeat(d_skip.astype(F32), SSM_HEAD_DIM).reshape(1, d_inner)
    row = lambda w: pl.BlockSpec((tm, w), lambda i: (i, 0))
    return pl.pallas_call(
        _mamba_finish_kernel,
        grid=(ntiles,),
        in_specs=[row(d_inner), row(d_inner), row(d_inner), row(d_inner), row(d_model),
                  _mod_spec(d_model, nct), _resident((1, d_inner)), _resident((1, d_inner)),
                  _resident(w_out.shape), _resident((1, d_model)), _resident((1, d_model))],
        out_specs=row(d_model),
        out_shape=jax.ShapeDtypeStruct((n_rows, d_model), F32),
        name="mamba_finish",
    )(yf, yb, xbc, z, X, mod2, drep, norm_g.reshape(1, d_inner), w_out.astype(BF16),
      ln_g.reshape(1, d_model), ln_b.reshape(1, d_model))


def _mamba_layer(X, mod2, w_in, conv_w, conv_b, dt_bias, a_log, d_skip, norm_g, w_out, ln_g, ln_b, nct_rows):
    z, xbc, dt, dtT = _mamba_inproj(X, mod2, w_in, conv_w, conv_b, dt_bias, nct_rows)
    yf, yb = _ssd_scan(xbc, dt, dtT, a_log, z.shape[1])
    return _mamba_finish(yf, yb, xbc, z, X, mod2, d_skip, norm_g, w_out, ln_g, ln_b, nct_rows)


def _attn_inproj_kernel(x_ref, mod_ref, w_ref, gm_ref, qn_ref, kn_ref, cos_ref, sin_ref,
                        qT_ref, k_ref, vT_ref, g_ref, *, wq, wkv):
    m = mod_ref[0]
    hx = (x_ref[...] * (1.0 + m[1:2]) + m[0:1]).astype(BF16)
    hd = ATTN_HEAD_DIM

    def norm_rope(t, gmean, gain):
        width = t.shape[1]
        ms = jnp.dot((t * t).astype(BF16), gmean, preferred_element_type=F32)
        tn = t * lax.rsqrt(ms + RMS_EPS) * gain
        reps = width // LANES
        cs = jnp.concatenate([cos_ref[...]] * reps, axis=1)
        sn = jnp.concatenate([sin_ref[...]] * reps, axis=1)
        lane = lax.broadcasted_iota(jnp.int32, (1, width), 1)
        first_half = (lane % hd) < hd // 2
        swapped = jnp.where(first_half, pltpu.roll(tn, width - hd // 2, 1), pltpu.roll(tn, hd // 2, 1))
        return tn * cs + swapped * sn

    q = jnp.dot(hx, w_ref[:, 0:wq], preferred_element_type=F32)
    qr = norm_rope(q, gm_ref[...], qn_ref[...]) * (hd ** -0.5 * LOG2_E)
    qT_ref[...] = qr.T.astype(BF16)
    k = jnp.dot(hx, w_ref[:, wq:wq + wkv], preferred_element_type=F32)
    kr = norm_rope(k, gm_ref[0:wkv, 0:wkv], kn_ref[...])
    for j in range(wkv // hd):
        k_ref[j] = kr[:, j * hd:(j + 1) * hd].astype(BF16)
    v = jnp.dot(hx, w_ref[:, wq + wkv:wq + 2 * wkv], preferred_element_type=F32)
    v_t = v.T.astype(BF16)
    for j in range(wkv // hd):
        vT_ref[j, 0:hd, :] = v_t[j * hd:(j + 1) * hd, :]
        vT_ref[j, hd:, :] = jnp.ones((V_ROWS - hd, v_t.shape[1]), BF16)
    g_ref[...] = jnp.dot(hx, w_ref[:, wq + 2 * wkv:], preferred_element_type=F32).astype(BF16)


def _rope_tables(n_tok, n_ctx):
    t = jnp.arange(n_tok)
    row_ids = (t // GRID_W).astype(F32)
    col_ids = (t % GRID_W).astype(F32)
    half = ATTN_HEAD_DIM // 2
    inv = ROPE_THETA ** (-jnp.arange(0, half, 2, dtype=F32) / half)
    ang = jnp.concatenate([row_ids[:, None] * inv, col_ids[:, None] * inv], axis=-1)
    c, s = jnp.cos(ang), jnp.sin(ang)
    cos_h = jnp.concatenate([c, c], axis=-1)
    sin_h = jnp.concatenate([-s, s], axis=-1)
    reps = LANES // ATTN_HEAD_DIM
    cos2 = jnp.concatenate([jnp.ones((n_ctx, LANES), F32), jnp.tile(cos_h, (1, reps))], axis=0)
    sin2 = jnp.concatenate([jnp.zeros((n_ctx, LANES), F32), jnp.tile(sin_h, (1, reps))], axis=0)
    return cos2, sin2


def _attn_inproj(X, mod2, w_in, q_norm, k_norm, cos2, sin2, nct_rows):
    n_rows, d_model = X.shape
    tm = ROW_TILE
    ntiles, nct = n_rows // tm, nct_rows // tm
    hd = ATTN_HEAD_DIM
    wq, wkv = ATTN_HEADS * hd, ATTN_KV_HEADS * hd
    idx = jnp.arange(wq) // hd
    gmean = (idx[:, None] == idx[None, :]).astype(BF16) * (1.0 / hd)
    qn = jnp.tile(q_norm.astype(F32), ATTN_HEADS).reshape(1, wq)
    kn = jnp.tile(k_norm.astype(F32), ATTN_KV_HEADS).reshape(1, wkv)
    w = w_in.astype(BF16)
    kern = functools.partial(_attn_inproj_kernel, wq=wq, wkv=wkv)
    return pl.pallas_call(
        kern,
        grid=(ntiles,),
        in_specs=[
            pl.BlockSpec((tm, d_model), lambda i: (i, 0)),
            _mod_spec(d_model, nct),
            _resident(w.shape), _resident(gmean.shape), _resident(qn.shape), _resident(kn.shape),
            pl.BlockSpec((tm, LANES), lambda i: (i, 0)),
            pl.BlockSpec((tm, LANES), lambda i: (i, 0)),
        ],
        out_specs=[
            pl.BlockSpec((wq, tm), lambda i: (0, i)),
            pl.BlockSpec((ATTN_KV_HEADS, tm, hd), lambda i: (0, i, 0)),
            pl.BlockSpec((ATTN_KV_HEADS, V_ROWS, tm), lambda i: (0, 0, i)),
            pl.BlockSpec((tm, wq), lambda i: (i, 0)),
        ],
        out_shape=[
            jax.ShapeDtypeStruct((wq, n_rows), BF16),
            jax.ShapeDtypeStruct((ATTN_KV_HEADS, n_rows, hd), BF16),
            jax.ShapeDtypeStruct((ATTN_KV_HEADS, V_ROWS, n_rows), BF16),
            jax.ShapeDtypeStruct((n_rows, wq), BF16),
        ],
        name="attn_inproj",
    )(X, mod2, w, gmean, qn, kn, cos2, sin2)


def _flash_kernel(qT_ref, k_ref, vT_ref, o_ref, s_scr, m_scr, acc_scr, *, bk, nct_tiles, nct_chunks, n_chunks):
    qi = pl.program_id(1)
    hd = ATTN_HEAD_DIM
    n_rep = ATTN_HEADS // ATTN_KV_HEADS
    m_scr[...] = jnp.full_like(m_scr, -jnp.inf)
    acc_scr[...] = jnp.zeros_like(acc_scr)
    nch = jnp.where(qi < nct_tiles, nct_chunks, n_chunks)

    def produce(buf, j):
        off = pl.multiple_of(jnp.minimum(j, nch - 1) * bk, bk)
        kb = k_ref[0, pl.ds(off, bk), :]
        for g in range(n_rep):
            s_scr[buf, g] = jnp.dot(kb, qT_ref[g * hd:(g + 1) * hd, :], preferred_element_type=F32)

    def consume(buf, j):
        off = pl.multiple_of(j * bk, bk)
        vb = vT_ref[0, :, pl.ds(off, bk)]
        for g in range(n_rep):
            s = s_scr[buf, g]
            m_old = m_scr[g]
            m_new = jnp.maximum(m_old, jnp.max(s, axis=0, keepdims=True))
            alpha = jnp.exp2(m_old - m_new)
            p = jnp.exp2(s - m_new).astype(BF16)
            acc_scr[g] = alpha * acc_scr[g] + jnp.dot(vb, p, preferred_element_type=F32)
            m_scr[g] = m_new

    produce(0, 0)

    def pair(t, carry):
        produce(1, 2 * t + 1)
        consume(0, 2 * t)
        produce(0, 2 * t + 2)
        consume(1, 2 * t + 1)
        return carry

    lax.fori_loop(0, nch // 2, pair, 0)

    @pl.when(nch % 2 == 1)
    def _():
        consume(0, nch - 1)

    o_t = jnp.concatenate([acc_scr[g, 0:hd] / acc_scr[g, hd:hd + 1] for g in range(n_rep)], axis=0)
    o_ref[...] = o_t.T.astype(BF16)


def _flash_attention(qT, k, vT, nct_rows):
    wq, n_rows = qT.shape
    hd = ATTN_HEAD_DIM
    n_rep = ATTN_HEADS // ATTN_KV_HEADS
    bq = bk = ROW_TILE
    kern = functools.partial(_flash_kernel, bk=bk, nct_tiles=nct_rows // bq, nct_chunks=nct_rows // bk,
                             n_chunks=n_rows // bk)
    return pl.pallas_call(
        kern,
        grid=(ATTN_KV_HEADS, n_rows // bq),
        in_specs=[
            pl.BlockSpec((n_rep * hd, bq), lambda h, i: (h, i)),
            pl.BlockSpec((1, n_rows, hd), lambda h, i: (h, 0, 0)),
            pl.BlockSpec((1, V_ROWS, n_rows), lambda h, i: (h, 0, 0)),
        ],
        out_specs=pl.BlockSpec((bq, n_rep * hd), lambda h, i: (i, h)),
        out_shape=jax.ShapeDtypeStruct((n_rows, wq), BF16),
        scratch_shapes=[
            pltpu.VMEM((2, n_rep, bk, bq), F32),
            pltpu.VMEM((n_rep, 1, bq), F32),
            pltpu.VMEM((n_rep, V_ROWS, bq), F32),
        ],
        name="flash_attention",
    )(qT, k, vT)


def _attn_out_kernel(o_ref, g_ref, x_ref, mod_ref, wout_ref, lng_ref, lnb_ref, out_ref):
    t = o_ref[...].astype(F32) * _silu(g_ref[...].astype(F32))
    out = jnp.dot(t.astype(BF16), wout_ref[...], preferred_element_type=F32)
    gate = mod_ref[0][2:3]
    out_ref[...] = _layer_norm(DEEPNORM_ALPHA * x_ref[...] + gate * out, lng_ref[...], lnb_ref[...])


def _attn_out(o, g, X, mod2, w_out, ln_g, ln_b, nct_rows):
    n_rows, d_model = X.shape
    wq = o.shape[1]
    tm = ROW_TILE
    ntiles, nct = n_rows // tm, nct_rows // tm
    row = lambda w: pl.BlockSpec((tm, w), lambda i: (i, 0))
    return pl.pallas_call(
        _attn_out_kernel,
        grid=(ntiles,),
        in_specs=[row(wq), row(wq), row(d_model), _mod_spec(d_model, nct), _resident(w_out.shape),
                  _resident((1, d_model)), _resident((1, d_model))],
        out_specs=row(d_model),
        out_shape=jax.ShapeDtypeStruct((n_rows, d_model), F32),
        name="attn_out",
    )(o, g, X, mod2, w_out.astype(BF16), ln_g.reshape(1, d_model), ln_b.reshape(1, d_model))


def _attn_layer(X, mod2, w_in, q_norm, k_norm, w_out, ln_g, ln_b, cos2, sin2, nct_rows):
    qT, k, vT, g = _attn_inproj(X, mod2, w_in, q_norm, k_norm, cos2, sin2, nct_rows)
    o = _flash_attention(qT, k, vT, nct_rows)
    return _attn_out(o, g, X, mod2, w_out, ln_g, ln_b, nct_rows)


def _pool_kernel(xp_ref, xm_ref, xn_ref, mod_ref, win_ref, gw_ref, ls_ref, wout_ref, lng_ref, lnb_ref,
                 o_ref, ext_scr, t_scr, *, tm, nct, ntiles, n_ctx, n_tok):
    i = pl.program_id(0)
    m = mod_ref[0]
    shift, scale, gate = m[0:1], m[1:2], m[2:3]

    def modulate(v):
        return v * (1.0 + scale) + shift

    x_m = xm_ref[...]
    hx_m = modulate(x_m)
    hx_ext = jnp.concatenate([modulate(xp_ref[...]), hx_m, modulate(xn_ref[...])], axis=0).astype(BF16)
    hx_mb = hx_m.astype(BF16)
    maskf = _halo_row_mask(i, tm, nct, ntiles)

    is_ctx = i < nct
    t0 = (i - jnp.where(is_ctx, 0, nct)) * tm
    n_seq = jnp.where(is_ctx, n_ctx, n_tok)
    pos = t0 + lax.broadcasted_iota(jnp.int32, (tm, 1), 0)

    n_groups = len(POOL_WINDOWS)
    gd = gw_ref.shape[1]
    width = n_groups * gd
    for gi, window in enumerate(POOL_WINDOWS):
        cs = slice(gi * gd, (gi + 1) * gd)
        ext_scr[...] = jnp.dot(hx_ext, win_ref[:, cs], preferred_element_type=F32) * maskf
        lo_off = window // 2
        hi_off = window - 1 - lo_off
        win_sum = ext_scr[pl.ds(HALO - lo_off, tm), :]
        for o in range(-lo_off + 1, hi_off + 1):
            win_sum = win_sum + ext_scr[pl.ds(HALO + o, tm), :]
        cnt = (jnp.minimum(pos + hi_off + 1, n_seq) - jnp.maximum(pos - lo_off, 0)).astype(F32)
        pooled = win_sum / cnt - ext_scr[pl.ds(HALO, tm), :]
        mixed = jnp.dot(pooled.astype(BF16), gw_ref[gi], preferred_element_type=F32)
        zc = jnp.dot(hx_mb, win_ref[:, width + gi * gd:width + (gi + 1) * gd], preferred_element_type=F32)
        t_scr[:, cs] = ((mixed * ls_ref[:, cs]) * _silu(zc)).astype(BF16)
    out = jnp.dot(t_scr[...], wout_ref[...], preferred_element_type=F32)
    o_ref[...] = _layer_norm(DEEPNORM_ALPHA * x_m + gate * out, lng_ref[...], lnb_ref[...])


def _pool_layer(X, mod2, w_in, group_w, layer_scale, w_out, ln_g, ln_b, nct_rows):
    n_rows, d_model = X.shape
    tm = ROW_TILE
    ntiles, nct = n_rows // tm, nct_rows // tm
    width = w_out.shape[0]
    gd = group_w.shape[1]
    assert max(POOL_WINDOWS) // 2 <= HALO
    prev, nxt = _halo_specs(d_model, tm, n_rows)
    kern = functools.partial(_pool_kernel, tm=tm, nct=nct, ntiles=ntiles, n_ctx=nct_rows,
                             n_tok=n_rows - nct_rows)
    return pl.pallas_call(
        kern,
        grid=(ntiles,),
        in_specs=[
            prev,
            pl.BlockSpec((tm, d_model), lambda i: (i, 0)),
            nxt,
            _mod_spec(d_model, nct),
            _resident(w_in.shape), _resident(group_w.shape), _resident((1, width)), _resident(w_out.shape),
            _resident((1, d_model)), _resident((1, d_model)),
        ],
        out_specs=pl.BlockSpec((tm, d_model), lambda i: (i, 0)),
        out_shape=jax.ShapeDtypeStruct((n_rows, d_model), F32),
        scratch_shapes=[pltpu.VMEM((tm + 2 * HALO, gd), F32), pltpu.VMEM((tm, width), BF16)],
        name="pool_mixer",
    )(X, X, X, mod2, w_in.astype(BF16), group_w.astype(BF16), layer_scale.reshape(1, width),
      w_out.astype(BF16), ln_g.reshape(1, d_model), ln_b.reshape(1, d_model))


def kernel(x, c, ctx, c_ctx, mod_w, mod_b, ln_g, ln_b, ssm_w_in, ssm_conv_w, ssm_conv_b, ssm_dt_bias,
           ssm_a_log, ssm_d, ssm_norm_g, ssm_w_out, attn_w_in, attn_q_norm, attn_k_norm, attn_w_out,
           pool_w_in, pool_group_w, pool_scale, pool_w_out):
    bsz, n_tok, d_model = x.shape
    n_ctx = ctx.shape[1]
    assert bsz == 1 and n_ctx % ROW_TILE == 0 and n_tok % ROW_TILE == 0
    X = jnp.concatenate([ctx[0], x[0]], axis=0)
    mods = _mod_vectors(c, c_ctx, mod_w, mod_b)
    cos2, sin2 = _rope_tables(n_tok, n_ctx)
    for i in range(DEPTH):
        kind, j = i % N_MIXERS, i // N_MIXERS
        m3 = mods[i, 0:2].reshape(2, 3, d_model)
        mod2 = jnp.zeros((2, 8, d_model), F32).at[:, 0:3].set(m3)
        if kind == 0:
            X = _mamba_layer(X, mod2, ssm_w_in[j], ssm_conv_w[j], ssm_conv_b[j], ssm_dt_bias[j],
                             ssm_a_log[j], ssm_d[j], ssm_norm_g[j], ssm_w_out[j], ln_g[i], ln_b[i], n_ctx)
        elif kind == 1:
            X = _attn_layer(X, mod2, attn_w_in[j], attn_q_norm[j], attn_k_norm[j], attn_w_out[j],
                            ln_g[i], ln_b[i], cos2, sin2, n_ctx)
        else:
            X = _pool_layer(X, mod2, pool_w_in[j], pool_group_w[j], pool_scale[j], pool_w_out[j],
                            ln_g[i], ln_b[i], n_ctx)
    return X[n_ctx:][None]
```

```python
import functools

import jax
import jax.numpy as jnp
from jax import lax
from jax.experimental import pallas as pl
from jax.experimental.pallas import tpu as pltpu

F32 = jnp.float32
BF16 = jnp.bfloat16
HIGHEST = lax.Precision.HIGHEST

DEPTH = 4
N_MIXERS = 3
GRID_W = 64
ROPE_THETA = 10000.0

SSM_HEAD_DIM = 64
SSM_STATE = 128
SSM_GROUPS = 8
SSM_CONV = 5
SSM_CHUNK = 256

ATTN_HEADS = 16
ATTN_KV_HEADS = 4
ATTN_HEAD_DIM = 64
V_ROWS = ATTN_HEAD_DIM + 16
LOG2_E = 1.4426950408889634
FLASH_UNROLL = 8

POOL_WINDOWS = (2, 4, 8, 16)

DEEPNORM_ALPHA = (2 * DEPTH) ** 0.25
LN_EPS = 1e-5
RMS_EPS = 1e-6

ROW_TILE = 256
HALO = 8
LANES = 128
COL_CHUNK = 512


def _silu(v):
    return v * jax.nn.sigmoid(v)


def _layer_norm(r, g, b):
    mu = jnp.mean(r, axis=-1, keepdims=True)
    d = r - mu
    var = jnp.mean(d * d, axis=-1, keepdims=True)
    return d * lax.rsqrt(var + LN_EPS) * g + b


def _resident(shape):
    nd = len(shape)
    return pl.BlockSpec(shape, lambda *_: (0,) * nd, pipeline_mode=pl.Buffered(1))


def _mod_spec(d_model, nct):
    return pl.BlockSpec((1, 8, d_model), lambda i: (jnp.where(i < nct, 1, 0), 0, 0))


def _halo_specs(d_model, tm, n_rows):
    per = tm // HALO
    last = n_rows // HALO - 1
    prev = pl.BlockSpec((HALO, d_model), lambda i: (jnp.maximum(i * per - 1, 0), 0))
    nxt = pl.BlockSpec((HALO, d_model), lambda i: (jnp.minimum((i + 1) * per, last), 0))
    return prev, nxt


def _halo_row_mask(i, tm, nct, ntiles):
    prev_ok = jnp.logical_and(i != 0, i != nct)
    next_ok = jnp.logical_and(i != nct - 1, i != ntiles - 1)
    lo = jnp.where(prev_ok, 0, HALO)
    hi = jnp.where(next_ok, tm + 2 * HALO, tm + HALO)
    rows = lax.broadcasted_iota(jnp.int32, (tm + 2 * HALO, 1), 0)
    return jnp.logical_and(rows >= lo, rows < hi).astype(F32)


def _mod_kernel(c_ref, w_ref, b_ref, o_ref):
    s = _silu(c_ref[...])
    o_ref[0] = jnp.dot(s, w_ref[0], precision=HIGHEST, preferred_element_type=F32) + b_ref[0]


def _mod_vectors(c, c_ctx, mod_w, mod_b):
    depth, d_model, d3 = mod_w.shape
    cpad = jnp.zeros((8, d_model), F32).at[0].set(c[0]).at[1].set(c_ctx)
    out = pl.pallas_call(
        _mod_kernel,
        grid=(depth, d3 // d_model),
        in_specs=[
            pl.BlockSpec((8, d_model), lambda i, j: (0, 0)),
            pl.BlockSpec((1, d_model, d_model), lambda i, j: (i, 0, j)),
            pl.BlockSpec((1, 1, d_model), lambda i, j: (i, 0, j)),
        ],
        out_specs=pl.BlockSpec((1, 8, d_model), lambda i, j: (i, 0, j)),
        out_shape=jax.ShapeDtypeStruct((depth, 8, d3), F32),
        name="mod_vectors",
    )(cpad, mod_w, mod_b.reshape(depth, 1, d3))
    return out


def _mamba_inproj_kernel(xp_ref, xm_ref, xn_ref, mod_ref, wz_ref, wxbc_ref, wdt_ref, cw_ref, cb_ref,
                         dtb_ref, z_ref, xbc_ref, dt_ref, dtT_ref, *, tm, nct, ntiles):
    i = pl.program_id(0)
    m = mod_ref[0]
    shift, scale = m[0:1], m[1:2]

    def modulate(v):
        return v * (1.0 + scale) + shift

    hx_m = modulate(xm_ref[...])
    hx_ext = jnp.concatenate([modulate(xp_ref[...]), hx_m, modulate(xn_ref[...])], axis=0).astype(BF16)
    hx_mb = hx_m.astype(BF16)
    maskf = _halo_row_mask(i, tm, nct, ntiles)
    pad = SSM_CONV // 2

    n_xbc = wxbc_ref.shape[1]
    for c in range(n_xbc // COL_CHUNK):
        cs = slice(c * COL_CHUNK, (c + 1) * COL_CHUNK)
        pre = jnp.dot(hx_ext, wxbc_ref[:, cs], preferred_element_type=F32) * maskf
        n_ext = pre.shape[0]
        acc = cb_ref[:, cs] + cw_ref[pad:pad + 1, cs] * pre[HALO:HALO + tm]
        for k in range(SSM_CONV):
            if k != pad:
                acc = acc + cw_ref[k:k + 1, cs] * pltpu.roll(pre, (pad - k) % n_ext, 0)[HALO:HALO + tm]
        xbc_ref[:, cs] = _silu(acc).astype(BF16)

    n_z = wz_ref.shape[1]
    for c in range(n_z // COL_CHUNK):
        cs = slice(c * COL_CHUNK, (c + 1) * COL_CHUNK)
        z_ref[:, cs] = jnp.dot(hx_mb, wz_ref[:, cs], preferred_element_type=F32).astype(BF16)

    dt_raw = jnp.dot(hx_m, wdt_ref[...], precision=HIGHEST, preferred_element_type=F32) + dtb_ref[...]
    dt = jnp.maximum(dt_raw, 0.0) + jnp.log1p(jnp.exp(-jnp.abs(dt_raw)))
    dt_ref[...] = dt
    dtT_ref[...] = dt.T


def _mamba_inproj(X, mod2, w_in, conv_w, conv_b, dt_bias, nct_rows):
    n_rows, d_model = X.shape
    tm = ROW_TILE
    ntiles, nct = n_rows // tm, nct_rows // tm
    d_inner = w_in.shape[1] - conv_w.shape[1] - 2 * (dt_bias.shape[-1])
    n_xbc = conv_w.shape[1]
    n_dt = 2 * dt_bias.shape[-1]
    wz = w_in[:, :d_inner].astype(BF16)
    wxbc = w_in[:, d_inner:d_inner + n_xbc].astype(BF16)
    wdt = jnp.zeros((d_model, LANES), F32).at[:, :n_dt].set(w_in[:, d_inner + n_xbc:])
    dtb = jnp.zeros((1, LANES), F32).at[0, :n_dt].set(dt_bias.reshape(-1))
    cw = jnp.zeros((8, n_xbc), F32).at[:SSM_CONV].set(conv_w)
    prev, nxt = _halo_specs(d_model, tm, n_rows)
    kern = functools.partial(_mamba_inproj_kernel, tm=tm, nct=nct, ntiles=ntiles)
    return pl.pallas_call(
        kern,
        grid=(ntiles,),
        in_specs=[
            prev,
            pl.BlockSpec((tm, d_model), lambda i: (i, 0)),
            nxt,
            _mod_spec(d_model, nct),
            _resident(wz.shape), _resident(wxbc.shape), _resident(wdt.shape),
            _resident(cw.shape), _resident((1, n_xbc)), _resident(dtb.shape),
        ],
        out_specs=[
            pl.BlockSpec((tm, d_inner), lambda i: (i, 0)),
            pl.BlockSpec((tm, n_xbc), lambda i: (i, 0)),
            pl.BlockSpec((tm, LANES), lambda i: (i, 0)),
            pl.BlockSpec((LANES, tm), lambda i: (0, i)),
        ],
        out_shape=[
            jax.ShapeDtypeStruct((n_rows, d_inner), BF16),
            jax.ShapeDtypeStruct((n_rows, n_xbc), BF16),
            jax.ShapeDtypeStruct((n_rows, LANES), F32),
            jax.ShapeDtypeStruct((LANES, n_rows), F32),
        ],
        name="mamba_inproj",
    )(X, X, X, mod2, wz, wxbc, wdt, cw, conv_b.reshape(1, n_xbc), dtb)


def _split3(v):
    hi = v.astype(BF16)
    r1 = v - hi.astype(F32)
    mid = r1.astype(BF16)
    lo = (r1 - mid.astype(F32)).astype(BF16)
    return hi, mid, lo


def _ssd_direction(d, xs_ref, b_ref, c_ref, dt_ref, dtT_ref, arow_ref, acol_ref, r2_ref, y_ref,
                   state_scr, lower, upper, lane_head, *, L, n_heads):
    n_groups = SSM_GROUPS
    hpg = n_heads // n_groups
    gw = hpg * SSM_HEAD_DIM
    ns = SSM_STATE
    dt = dt_ref[...]
    dA = dt * arow_ref[0:1, :]
    dAT = dtT_ref[...] * acol_ref[...]
    low3, up3 = (jnp.concatenate([t.astype(BF16)] * 3, axis=1) for t in (lower, upper))
    low3v, up3v = (jnp.concatenate([t.astype(BF16)] * 3, axis=0) for t in (lower, upper))
    dA3 = jnp.concatenate(_split3(dA), axis=0)
    dAT3 = jnp.concatenate(_split3(dAT), axis=1)
    if d == 0:
        cum = jnp.dot(low3, dA3, preferred_element_type=F32)
        cumT = jnp.dot(dAT3, up3v, preferred_element_type=F32)
        tot = cum[L - 1:L]
        mask = lower
    else:
        cum = jnp.dot(up3, dA3, preferred_element_type=F32)
        cumT = jnp.dot(dAT3, low3v, preferred_element_type=F32)
        tot = cum[0:1]
        mask = upper
    e_in = jnp.exp(cum)
    tail = jnp.exp(tot - cum)
    st = jnp.concatenate([dt, dt * tail, e_in], axis=0)
    hi = st.astype(BF16)
    lo = (st - hi.astype(F32)).astype(BF16)
    lhs = jnp.concatenate([hi, lo], axis=1)
    neg_inf = jnp.float32(-jnp.inf)

    for g in range(n_groups):
        gs = slice(g * gw, (g + 1) * gw)
        rep = jnp.dot(lhs, r2_ref[:, gs], preferred_element_type=F32)
        xs_g = xs_ref[:, gs].astype(F32)
        xdt = xs_g * rep[0:L]
        xdt_tail = (xs_g * rep[L:2 * L]).astype(BF16)
        e_rep = rep[2 * L:3 * L]
        bg = b_ref[:, g * ns:(g + 1) * ns]
        cg = c_ref[:, g * ns:(g + 1) * ns]
        cb = lax.dot_general(cg, bg, (((1,), (1,)), ((), ())), preferred_element_type=F32)
        y = None
        for k in range(hpg):
            lane = d * n_heads + g * hpg + k
            seg = cum[:, lane:lane + 1] - cumT[lane:lane + 1, :]
            dec = jnp.exp(jnp.where(mask, seg, neg_inf))
            sc = (cb * dec).astype(BF16)
            xk = jnp.where(lane_head == k, xdt, 0.0).astype(BF16)
            yk = jnp.dot(sc, xk, preferred_element_type=F32)
            y = yk if y is None else y + yk
        h_t = state_scr[d, g]
        y_off = jnp.dot(cg, h_t.astype(BF16), preferred_element_type=F32) * e_rep
        e_tot = e_rep[L - 1:L] if d == 0 else e_rep[0:1]
        upd = lax.dot_general(bg, xdt_tail, (((0,), (0,)), ((), ())), preferred_element_type=F32)
        state_scr[d, g] = h_t * e_tot + upd
        y_ref[:, gs] = (y + y_off).astype(BF16)


def _ssd_kernel(xsf, bf, cf, dtf, dtTf, xsb, bb, cb_, dtb, dtTb, arow_ref, acol_ref, r2f_ref, r2b_ref,
                yf_ref, yb_ref, state_scr, *, L, n_heads):
    @pl.when(pl.program_id(0) == 0)
    def _():
        state_scr[...] = jnp.zeros_like(state_scr)

    r = lax.broadcasted_iota(jnp.int32, (L, L), 0)
    c = lax.broadcasted_iota(jnp.int32, (L, L), 1)
    lower, upper = c <= r, r <= c
    gw = (n_heads // SSM_GROUPS) * SSM_HEAD_DIM
    lane_head = lax.broadcasted_iota(jnp.int32, (1, gw), 1) // SSM_HEAD_DIM
    common = dict(L=L, n_heads=n_heads)
    _ssd_direction(0, xsf, bf, cf, dtf, dtTf, arow_ref, acol_ref, r2f_ref, yf_ref, state_scr,
                   lower, upper, lane_head, **common)
    _ssd_direction(1, xsb, bb, cb_, dtb, dtTb, arow_ref, acol_ref, r2b_ref, yb_ref, state_scr,
                   lower, upper, lane_head, **common)


def _ssd_scan(xbc, dt, dtT, a_log, d_inner):
    n_rows = xbc.shape[0]
    L = SSM_CHUNK
    n = n_rows // L
    n_heads = a_log.shape[-1]
    n_bc = SSM_GROUPS * SSM_STATE
    a = -jnp.exp(a_log.astype(F32)).reshape(-1)
    a_pad = jnp.zeros((LANES,), F32).at[:2 * n_heads].set(a)
    arow = jnp.zeros((8, LANES), F32).at[0].set(a_pad)
    acol = jnp.broadcast_to(a_pad[:, None], (LANES, L))
    head_of_col = jnp.arange(d_inner) // SSM_HEAD_DIM
    r2 = []
    for d in range(2):
        sel = (jnp.arange(LANES)[:, None] == (d * n_heads + head_of_col)[None, :]).astype(BF16)
        r2.append(jnp.concatenate([sel, sel], axis=0))
    fwd = lambda i: i
    bwd = lambda i: jnp.where(i == 0, 0, n - i)
    assert n_rows // L >= 2 and L == ROW_TILE

    def specs(cm):
        return [
            pl.BlockSpec((L, d_inner), lambda i: (cm(i), 0)),
            pl.BlockSpec((L, n_bc), lambda i: (cm(i), d_inner // n_bc)),
            pl.BlockSpec((L, n_bc), lambda i: (cm(i), d_inner // n_bc + 1)),
            pl.BlockSpec((L, LANES), lambda i: (cm(i), 0)),
            pl.BlockSpec((LANES, L), lambda i: (0, cm(i))),
        ]

    kern = functools.partial(_ssd_kernel, L=L, n_heads=n_heads)
    return pl.pallas_call(
        kern,
        grid=(n,),
        in_specs=specs(fwd) + specs(bwd) + [
            _resident(arow.shape), _resident(acol.shape), _resident(r2[0].shape), _resident(r2[1].shape)],
        out_specs=[
            pl.BlockSpec((L, d_inner), lambda i: (fwd(i), 0)),
            pl.BlockSpec((L, d_inner), lambda i: (bwd(i), 0)),
        ],
        out_shape=[jax.ShapeDtypeStruct((n_rows, d_inner), BF16)] * 2,
        scratch_shapes=[pltpu.VMEM((2, SSM_GROUPS, SSM_STATE, d_inner // SSM_GROUPS), F32)],
        compiler_params=pltpu.CompilerParams(dimension_semantics=("arbitrary",)),
        name="ssd_scan",
    )(xbc, xbc, xbc, dt, dtT, xbc, xbc, xbc, dt, dtT, arow, acol, r2[0], r2[1])


def _mamba_finish_kernel(yf_ref, yb_ref, xs_ref, z_ref, x_ref, mod_ref, drep_ref, ng_ref, wout_ref,
                         lng_ref, lnb_ref, o_ref):
    y = yf_ref[...].astype(F32) + yb_ref[...].astype(F32) + xs_ref[...].astype(F32) * drep_ref[...]
    t = y * _silu(z_ref[...].astype(F32))
    t = t * lax.rsqrt(jnp.mean(t * t, axis=-1, keepdims=True) + RMS_EPS) * ng_ref[...]
    out = jnp.dot(t.astype(BF16), wout_ref[...], preferred_element_type=F32)
    gate = mod_ref[0][2:3]
    o_ref[...] = _layer_norm(DEEPNORM_ALPHA * x_ref[...] + gate * out, lng_ref[...], lnb_ref[...])


def _mamba_finish(yf, yb, xbc, z, X, mod2, d_skip, norm_g, w_out, ln_g, ln_b, nct_rows, latent_only):
    n_rows, d_model = X.shape
    d_inner = z.shape[1]
    tm = ROW_TILE
    ntiles, nct = n_rows // tm, nct_rows // tm
    skip = nct if latent_only else 0
    drep = jnp.repeat(d_skip.astype(F32), SSM_HEAD_DIM).reshape(1, d_inner)
    row = lambda w: pl.BlockSpec((tm, w), lambda i: (i + skip, 0))
    return pl.pallas_call(
        _mamba_finish_kernel,
        grid=(ntiles - skip,),
        in_specs=[row(d_inner), row(d_inner), row(d_inner), row(d_inner), row(d_model),
                  _mod_spec(d_model, nct - skip), _resident((1, d_inner)), _resident((1, d_inner)),
                  _resident(w_out.shape), _resident((1, d_model)), _resident((1, d_model))],
        out_specs=pl.BlockSpec((tm, d_model), lambda i: (i, 0)),
        out_shape=jax.ShapeDtypeStruct((n_rows - skip * tm, d_model), F32),
        name="mamba_finish",
    )(yf, yb, xbc, z, X, mod2, drep, norm_g.reshape(1, d_inner), w_out.astype(BF16),
      ln_g.reshape(1, d_model), ln_b.reshape(1, d_model))


def _mamba_layer(X, mod2, w_in, conv_w, conv_b, dt_bias, a_log, d_skip, norm_g, w_out, ln_g, ln_b, nct_rows,
                 latent_only):
    z, xbc, dt, dtT = _mamba_inproj(X, mod2, w_in, conv_w, conv_b, dt_bias, nct_rows)
    yf, yb = _ssd_scan(xbc, dt, dtT, a_log, z.shape[1])
    return _mamba_finish(yf, yb, xbc, z, X, mod2, d_skip, norm_g, w_out, ln_g, ln_b, nct_rows, latent_only)


def _attn_inproj_kernel(x_ref, mod_ref, w_ref, gm_ref, qn_ref, kn_ref, cos_ref, sin_ref,
                        qT_ref, k_ref, vT_ref, g_ref, *, wq, wkv):
    m = mod_ref[0]
    hx = (x_ref[...] * (1.0 + m[1:2]) + m[0:1]).astype(BF16)
    hd = ATTN_HEAD_DIM

    def norm_rope(t, gmean, gain):
        width = t.shape[1]
        ms = jnp.dot((t * t).astype(BF16), gmean, preferred_element_type=F32)
        tn = t * lax.rsqrt(ms + RMS_EPS) * gain
        reps = width // LANES
        cs = jnp.concatenate([cos_ref[...]] * reps, axis=1)
        sn = jnp.concatenate([sin_ref[...]] * reps, axis=1)
        lane = lax.broadcasted_iota(jnp.int32, (1, width), 1)
        first_half = (lane % hd) < hd // 2
        swapped = jnp.where(first_half, pltpu.roll(tn, width - hd // 2, 1), pltpu.roll(tn, hd // 2, 1))
        return tn * cs + swapped * sn

    q = jnp.dot(hx, w_ref[:, 0:wq], preferred_element_type=F32)
    qr = norm_rope(q, gm_ref[...], qn_ref[...]) * (hd ** -0.5 * LOG2_E)
    qT_ref[...] = qr.T.astype(BF16)
    k = jnp.dot(hx, w_ref[:, wq:wq + wkv], preferred_element_type=F32)
    kr = norm_rope(k, gm_ref[0:wkv, 0:wkv], kn_ref[...])
    for j in range(wkv // hd):
        k_ref[j] = kr[:, j * hd:(j + 1) * hd].astype(BF16)
    v = jnp.dot(hx, w_ref[:, wq + wkv:wq + 2 * wkv], preferred_element_type=F32)
    v_t = v.T.astype(BF16)
    for j in range(wkv // hd):
        vT_ref[j, 0:hd, :] = v_t[j * hd:(j + 1) * hd, :]
        vT_ref[j, hd:, :] = jnp.ones((V_ROWS - hd, v_t.shape[1]), BF16)
    g_ref[...] = jnp.dot(hx, w_ref[:, wq + 2 * wkv:], preferred_element_type=F32).astype(BF16)


def _rope_tables(n_tok, n_ctx):
    t = jnp.arange(n_tok)
    row_ids = (t // GRID_W).astype(F32)
    col_ids = (t % GRID_W).astype(F32)
    half = ATTN_HEAD_DIM // 2
    inv = ROPE_THETA ** (-jnp.arange(0, half, 2, dtype=F32) / half)
    ang = jnp.concatenate([row_ids[:, None] * inv, col_ids[:, None] * inv], axis=-1)
    c, s = jnp.cos(ang), jnp.sin(ang)
    cos_h = jnp.concatenate([c, c], axis=-1)
    sin_h = jnp.concatenate([-s, s], axis=-1)
    reps = LANES // ATTN_HEAD_DIM
    cos2 = jnp.concatenate([jnp.ones((n_ctx, LANES), F32), jnp.tile(cos_h, (1, reps))], axis=0)
    sin2 = jnp.concatenate([jnp.zeros((n_ctx, LANES), F32), jnp.tile(sin_h, (1, reps))], axis=0)
    return cos2, sin2


def _attn_inproj(X, mod2, w_in, q_norm, k_norm, cos2, sin2, nct_rows):
    n_rows, d_model = X.shape
    tm = ROW_TILE
    ntiles, nct = n_rows // tm, nct_rows // tm
    hd = ATTN_HEAD_DIM
    wq, wkv = ATTN_HEADS * hd, ATTN_KV_HEADS * hd
    idx = jnp.arange(wq) // hd
    gmean = (idx[:, None] == idx[None, :]).astype(BF16) * (1.0 / hd)
    qn = jnp.tile(q_norm.astype(F32), ATTN_HEADS).reshape(1, wq)
    kn = jnp.tile(k_norm.astype(F32), ATTN_KV_HEADS).reshape(1, wkv)
    w = w_in.astype(BF16)
    kern = functools.partial(_attn_inproj_kernel, wq=wq, wkv=wkv)
    return pl.pallas_call(
        kern,
        grid=(ntiles,),
        in_specs=[
            pl.BlockSpec((tm, d_model), lambda i: (i, 0)),
            _mod_spec(d_model, nct),
            _resident(w.shape), _resident(gmean.shape), _resident(qn.shape), _resident(kn.shape),
            pl.BlockSpec((tm, LANES), lambda i: (i, 0)),
            pl.BlockSpec((tm, LANES), lambda i: (i, 0)),
        ],
        out_specs=[
            pl.BlockSpec((wq, tm), lambda i: (0, i)),
            pl.BlockSpec((ATTN_KV_HEADS, tm, hd), lambda i: (0, i, 0)),
            pl.BlockSpec((ATTN_KV_HEADS, V_ROWS, tm), lambda i: (0, 0, i)),
            pl.BlockSpec((tm, wq), lambda i: (i, 0)),
        ],
        out_shape=[
            jax.ShapeDtypeStruct((wq, n_rows), BF16),
            jax.ShapeDtypeStruct((ATTN_KV_HEADS, n_rows, hd), BF16),
            jax.ShapeDtypeStruct((ATTN_KV_HEADS, V_ROWS, n_rows), BF16),
            jax.ShapeDtypeStruct((n_rows, wq), BF16),
        ],
        name="attn_inproj",
    )(X, mod2, w, gmean, qn, kn, cos2, sin2)


def _flash_kernel(qT_ref, k_ref, vT_ref, o_ref, s_scr, m_scr, acc_scr, *, bk, nct_tiles, nct_chunks, n_chunks):
    qi = pl.program_id(1)
    hd = ATTN_HEAD_DIM
    n_rep = ATTN_HEADS // ATTN_KV_HEADS
    m_scr[...] = jnp.full_like(m_scr, -jnp.inf)
    acc_scr[...] = jnp.zeros_like(acc_scr)
    nch = jnp.where(qi < nct_tiles, nct_chunks, n_chunks)

    def produce(buf, j):
        off = pl.multiple_of(jnp.minimum(j, nch - 1) * bk, bk)
        kb = k_ref[0, pl.ds(off, bk), :]
        for g in range(n_rep):
            s_scr[buf, g] = jnp.dot(kb, qT_ref[g * hd:(g + 1) * hd, :], preferred_element_type=F32)

    def consume(buf, j):
        off = pl.multiple_of(j * bk, bk)
        vb = vT_ref[0, :, pl.ds(off, bk)]
        for g in range(n_rep):
            s = s_scr[buf, g]
            m_old = m_scr[g]
            m_new = jnp.maximum(m_old, jnp.max(s, axis=0, keepdims=True))
            alpha = jnp.exp2(m_old - m_new)
            p = jnp.exp2(s - m_new).astype(BF16)
            acc_scr[g] = alpha * acc_scr[g] + jnp.dot(vb, p, preferred_element_type=F32)
            m_scr[g] = m_new

    produce(0, 0)
    n_trips = nch // FLASH_UNROLL

    def trip(t, carry):
        for u in range(FLASH_UNROLL):
            j = FLASH_UNROLL * t + u
            produce((u + 1) % 2, j + 1)
            consume(u % 2, j)
        return carry

    lax.fori_loop(0, n_trips, trip, 0)
    done = n_trips * FLASH_UNROLL

    @pl.when(done < nch)
    def _():
        consume(0, done)

    def leftover(j, carry):
        produce(0, j)
        consume(0, j)
        return carry

    lax.fori_loop(done + 1, nch, leftover, 0)

    o_t = jnp.concatenate([acc_scr[g, 0:hd] / acc_scr[g, hd:hd + 1] for g in range(n_rep)], axis=0)
    o_ref[...] = o_t.T.astype(BF16)


def _flash_attention(qT, k, vT, nct_rows):
    wq, n_rows = qT.shape
    hd = ATTN_HEAD_DIM
    n_rep = ATTN_HEADS // ATTN_KV_HEADS
    bq = bk = ROW_TILE
    kern = functools.partial(_flash_kernel, bk=bk, nct_tiles=nct_rows // bq, nct_chunks=nct_rows // bk,
                             n_chunks=n_rows // bk)
    return pl.pallas_call(
        kern,
        grid=(ATTN_KV_HEADS, n_rows // bq),
        in_specs=[
            pl.BlockSpec((n_rep * hd, bq), lambda h, i: (h, i)),
            pl.BlockSpec((1, n_rows, hd), lambda h, i: (h, 0, 0)),
            pl.BlockSpec((1, V_ROWS, n_rows), lambda h, i: (h, 0, 0)),
        ],
        out_specs=pl.BlockSpec((bq, n_rep * hd), lambda h, i: (i, h)),
        out_shape=jax.ShapeDtypeStruct((n_rows, wq), BF16),
        scratch_shapes=[
            pltpu.VMEM((2, n_rep, bk, bq), F32),
            pltpu.VMEM((n_rep, 1, bq), F32),
            pltpu.VMEM((n_rep, V_ROWS, bq), F32),
        ],
        name="flash_attention",
    )(qT, k, vT)


def _attn_out_kernel(o_ref, g_ref, x_ref, mod_ref, wout_ref, lng_ref, lnb_ref, out_ref):
    t = o_ref[...].astype(F32) * _silu(g_ref[...].astype(F32))
    out = jnp.dot(t.astype(BF16), wout_ref[...], preferred_element_type=F32)
    gate = mod_ref[0][2:3]
    out_ref[...] = _layer_norm(DEEPNORM_ALPHA * x_ref[...] + gate * out, lng_ref[...], lnb_ref[...])


def _attn_out(o, g, X, mod2, w_out, ln_g, ln_b, nct_rows):
    n_rows, d_model = X.shape
    wq = o.shape[1]
    tm = ROW_TILE
    ntiles, nct = n_rows // tm, nct_rows // tm
    row = lambda w: pl.BlockSpec((tm, w), lambda i: (i, 0))
    return pl.pallas_call(
        _attn_out_kernel,
        grid=(ntiles,),
        in_specs=[row(wq), row(wq), row(d_model), _mod_spec(d_model, nct), _resident(w_out.shape),
                  _resident((1, d_model)), _resident((1, d_model))],
        out_specs=row(d_model),
        out_shape=jax.ShapeDtypeStruct((n_rows, d_model), F32),
        name="attn_out",
    )(o, g, X, mod2, w_out.astype(BF16), ln_g.reshape(1, d_model), ln_b.reshape(1, d_model))


def _attn_layer(X, mod2, w_in, q_norm, k_norm, w_out, ln_g, ln_b, cos2, sin2, nct_rows):
    qT, k, vT, g = _attn_inproj(X, mod2, w_in, q_norm, k_norm, cos2, sin2, nct_rows)
    o = _flash_attention(qT, k, vT, nct_rows)
    return _attn_out(o, g, X, mod2, w_out, ln_g, ln_b, nct_rows)


def _pool_kernel(xp_ref, xm_ref, xn_ref, mod_ref, win_ref, gw_ref, ls_ref, wout_ref, lng_ref, lnb_ref,
                 o_ref, ext_scr, t_scr, *, tm, nct, ntiles, n_ctx, n_tok):
    i = pl.program_id(0)
    m = mod_ref[0]
    shift, scale, gate = m[0:1], m[1:2], m[2:3]

    def modulate(v):
        return v * (1.0 + scale) + shift

    x_m = xm_ref[...]
    hx_m = modulate(x_m)
    hx_ext = jnp.concatenate([modulate(xp_ref[...]), hx_m, modulate(xn_ref[...])], axis=0).astype(BF16)
    hx_mb = hx_m.astype(BF16)
    maskf = _halo_row_mask(i, tm, nct, ntiles)

    is_ctx = i < nct
    t0 = (i - jnp.where(is_ctx, 0, nct)) * tm
    n_seq = jnp.where(is_ctx, n_ctx, n_tok)
    pos = t0 + lax.broadcasted_iota(jnp.int32, (tm, 1), 0)

    n_groups = len(POOL_WINDOWS)
    gd = gw_ref.shape[1]
    width = n_groups * gd
    for gi, window in enumerate(POOL_WINDOWS):
        cs = slice(gi * gd, (gi + 1) * gd)
        ext_scr[...] = jnp.dot(hx_ext, win_ref[:, cs], preferred_element_type=F32) * maskf
        lo_off = window // 2
        hi_off = window - 1 - lo_off
        win_sum = ext_scr[pl.ds(HALO - lo_off, tm), :]
        for o in range(-lo_off + 1, hi_off + 1):
            win_sum = win_sum + ext_scr[pl.ds(HALO + o, tm), :]
        cnt = (jnp.minimum(pos + hi_off + 1, n_seq) - jnp.maximum(pos - lo_off, 0)).astype(F32)
        pooled = win_sum / cnt - ext_scr[pl.ds(HALO, tm), :]
        mixed = jnp.dot(pooled.astype(BF16), gw_ref[gi], preferred_element_type=F32)
        zc = jnp.dot(hx_mb, win_ref[:, width + gi * gd:width + (gi + 1) * gd], preferred_element_type=F32)
        t_scr[:, cs] = ((mixed * ls_ref[:, cs]) * _silu(zc)).astype(BF16)
    out = jnp.dot(t_scr[...], wout_ref[...], preferred_element_type=F32)
    o_ref[...] = _layer_norm(DEEPNORM_ALPHA * x_m + gate * out, lng_ref[...], lnb_ref[...])


def _pool_layer(X, mod2, w_in, group_w, layer_scale, w_out, ln_g, ln_b, nct_rows):
    n_rows, d_model = X.shape
    tm = ROW_TILE
    ntiles, nct = n_rows // tm, nct_rows // tm
    width = w_out.shape[0]
    gd = group_w.shape[1]
    assert max(POOL_WINDOWS) // 2 <= HALO
    prev, nxt = _halo_specs(d_model, tm, n_rows)
    kern = functools.partial(_pool_kernel, tm=tm, nct=nct, ntiles=ntiles, n_ctx=nct_rows,
                             n_tok=n_rows - nct_rows)
    return pl.pallas_call(
        kern,
        grid=(ntiles,),
        in_specs=[
            prev,
            pl.BlockSpec((tm, d_model), lambda i: (i, 0)),
            nxt,
            _mod_spec(d_model, nct),
            _resident(w_in.shape), _resident(group_w.shape), _resident((1, width)), _resident(w_out.shape),
            _resident((1, d_model)), _resident((1, d_model)),
        ],
        out_specs=pl.BlockSpec((tm, d_model), lambda i: (i, 0)),
        out_shape=jax.ShapeDtypeStruct((n_rows, d_model), F32),
        scratch_shapes=[pltpu.VMEM((tm + 2 * HALO, gd), F32), pltpu.VMEM((tm, width), BF16)],
        name="pool_mixer",
    )(X, X, X, mod2, w_in.astype(BF16), group_w.astype(BF16), layer_scale.reshape(1, width),
      w_out.astype(BF16), ln_g.reshape(1, d_model), ln_b.reshape(1, d_model))


def kernel(x, c, ctx, c_ctx, mod_w, mod_b, ln_g, ln_b, ssm_w_in, ssm_conv_w, ssm_conv_b, ssm_dt_bias,
           ssm_a_log, ssm_d, ssm_norm_g, ssm_w_out, attn_w_in, attn_q_norm, attn_k_norm, attn_w_out,
           pool_w_in, pool_group_w, pool_scale, pool_w_out):
    bsz, n_tok, d_model = x.shape
    n_ctx = ctx.shape[1]
    assert bsz == 1 and n_ctx % ROW_TILE == 0 and n_tok % ROW_TILE == 0
    X = jnp.concatenate([ctx[0], x[0]], axis=0)
    mods = _mod_vectors(c, c_ctx, mod_w, mod_b)
    cos2, sin2 = _rope_tables(n_tok, n_ctx)
    for i in range(DEPTH):
        kind, j = i % N_MIXERS, i // N_MIXERS
        m3 = mods[i, 0:2].reshape(2, 3, d_model)
        mod2 = jnp.zeros((2, 8, d_model), F32).at[:, 0:3].set(m3)
        if kind == 0:
            X = _mamba_layer(X, mod2, ssm_w_in[j], ssm_conv_w[j], ssm_conv_b[j], ssm_dt_bias[j],
                             ssm_a_log[j], ssm_d[j], ssm_norm_g[j], ssm_w_out[j], ln_g[i], ln_b[i], n_ctx,
                             latent_only=(i == DEPTH - 1))
        elif kind == 1:
            X = _attn_layer(X, mod2, attn_w_in[j], attn_q_norm[j], attn_k_norm[j], attn_w_out[j],
                            ln_g[i], ln_b[i], cos2, sin2, n_ctx)
        else:
            X = _pool_layer(X, mod2, pool_w_in[j], pool_group_w[j], pool_scale[j], pool_w_out[j],
                            ln_g[i], ln_b[i], n_ctx)
    if (DEPTH - 1) % N_MIXERS != 0:
        X = X[n_ctx:]
    return X[None]
```

```python
import functools

import jax
import jax.numpy as jnp
from jax import lax
from jax.experimental import pallas as pl
from jax.experimental.pallas import tpu as pltpu

F32 = jnp.float32
BF16 = jnp.bfloat16
HIGHEST = lax.Precision.HIGHEST

DEPTH = 4
N_MIXERS = 3
GRID_W = 64
ROPE_THETA = 10000.0

SSM_HEAD_DIM = 64
SSM_STATE = 128
SSM_GROUPS = 8
SSM_CONV = 5
SSM_CHUNK = 256

ATTN_HEADS = 16
ATTN_KV_HEADS = 4
ATTN_HEAD_DIM = 64
V_ROWS = ATTN_HEAD_DIM + 16
LOG2_E = 1.4426950408889634
SAFE_LOG2_SCORE = 60.0
FLASH_UNROLL = 8
FLASH_UNROLL_BOUNDED = 16

POOL_WINDOWS = (2, 4, 8, 16)

DEEPNORM_ALPHA = (2 * DEPTH) ** 0.25
LN_EPS = 1e-5
RMS_EPS = 1e-6

ROW_TILE = 256
HALO = 8
LANES = 128
COL_CHUNK = 512


def _silu(v):
    return v * jax.nn.sigmoid(v)


def _layer_norm(r, g, b):
    mu = jnp.mean(r, axis=-1, keepdims=True)
    d = r - mu
    var = jnp.mean(d * d, axis=-1, keepdims=True)
    return d * lax.rsqrt(var + LN_EPS) * g + b


def _resident(shape):
    nd = len(shape)
    return pl.BlockSpec(shape, lambda *_: (0,) * nd, pipeline_mode=pl.Buffered(1))


def _mod_spec(d_model, nct):
    return pl.BlockSpec((1, 8, d_model), lambda i: (jnp.where(i < nct, 1, 0), 0, 0))


def _halo_specs(d_model, tm, n_rows):
    per = tm // HALO
    last = n_rows // HALO - 1
    prev = pl.BlockSpec((HALO, d_model), lambda i: (jnp.maximum(i * per - 1, 0), 0))
    nxt = pl.BlockSpec((HALO, d_model), lambda i: (jnp.minimum((i + 1) * per, last), 0))
    return prev, nxt


def _halo_ok(i, nct, ntiles):
    prev_ok = jnp.logical_and(i != 0, i != nct)
    next_ok = jnp.logical_and(i != nct - 1, i != ntiles - 1)
    return prev_ok, next_ok


def _halo_row_mask(i, tm, nct, ntiles):
    prev_ok, next_ok = _halo_ok(i, nct, ntiles)
    lo = jnp.where(prev_ok, 0, HALO)
    hi = jnp.where(next_ok, tm + 2 * HALO, tm + HALO)
    rows = lax.broadcasted_iota(jnp.int32, (tm + 2 * HALO, 1), 0)
    return jnp.logical_and(rows >= lo, rows < hi).astype(F32)


def _mod_kernel(c_ref, w_ref, b_ref, o_ref):
    s = _silu(c_ref[...])
    o_ref[0] = jnp.dot(s, w_ref[0], precision=HIGHEST, preferred_element_type=F32) + b_ref[0]


def _mod_vectors(c, c_ctx, mod_w, mod_b):
    depth, d_model, d3 = mod_w.shape
    cpad = jnp.zeros((8, d_model), F32).at[0].set(c[0]).at[1].set(c_ctx)
    out = pl.pallas_call(
        _mod_kernel,
        grid=(depth, d3 // d_model),
        in_specs=[
            pl.BlockSpec((8, d_model), lambda i, j: (0, 0)),
            pl.BlockSpec((1, d_model, d_model), lambda i, j: (i, 0, j)),
            pl.BlockSpec((1, 1, d_model), lambda i, j: (i, 0, j)),
        ],
        out_specs=pl.BlockSpec((1, 8, d_model), lambda i, j: (i, 0, j)),
        out_shape=jax.ShapeDtypeStruct((depth, 8, d3), F32),
        name="mod_vectors",
    )(cpad, mod_w, mod_b.reshape(depth, 1, d3))
    return out


def _mamba_inproj_kernel(xp_ref, xm_ref, xn_ref, mod_ref, wz_ref, wxbc_ref, wdt_ref, cw_ref, cb_ref,
                         dtb_ref, z_ref, xbc_ref, dt_ref, dtT_ref, xs_scr, out_scr, *, tm, nct, ntiles):
    i = pl.program_id(0)
    m = mod_ref[0]
    shift, scale = m[0:1], m[1:2]

    def modulate(v):
        return v * (1.0 + scale) + shift

    hx_m = modulate(xm_ref[...])
    hx_mb = hx_m.astype(BF16)
    pad = SSM_CONV // 2

    nph = HALO
    per = tm // nph
    blk = per + HALO
    prev_ok, next_ok = _halo_ok(i, nct, ntiles)
    lead_row = lax.broadcasted_iota(jnp.int32, (HALO, 1), 0)
    next_row = jnp.where(next_ok, 0, -1)
    prev_row = jnp.where(prev_ok, HALO - 1, -1)
    n_slab = xm_ref.shape[1] // LANES
    for l in range(n_slab):
        xs_scr[l] = hx_m[:, l * LANES:(l + 1) * LANES]
    pieces = []
    for b in range(nph):
        lead = jnp.where(lead_row == next_row, modulate(xn_ref[b:b + 1, :]),
                         jnp.where(lead_row == prev_row, modulate(xp_ref[b:b + 1, :]), 0.0))
        body = jnp.concatenate([xs_scr[l, pl.ds(b, per, stride=nph), :] for l in range(n_slab)], axis=1)
        pieces += [lead, body]
    hx_perm = jnp.concatenate(pieces, axis=0).astype(BF16)

    n_xbc = wxbc_ref.shape[1]
    for c in range(n_xbc // COL_CHUNK):
        cs = slice(c * COL_CHUNK, (c + 1) * COL_CHUNK)
        pre = jnp.dot(hx_perm, wxbc_ref[:, cs], preferred_element_type=F32)
        blocks = {b: pre[b * blk:(b + 1) * blk] for b in range(nph)}
        for b in range(pad):
            blocks[b + nph] = pltpu.roll(blocks[b], blk - 1, 0)
            blocks[-1 - b] = pltpu.roll(blocks[nph - 1 - b], 1, 0)
        for b in range(nph):
            acc = cb_ref[:, cs] + cw_ref[0:1, cs] * blocks[b - pad][HALO:]
            for k in range(1, SSM_CONV):
                acc = acc + cw_ref[k:k + 1, cs] * blocks[b + k - pad][HALO:]
            act = _silu(acc)
            for l in range(COL_CHUNK // LANES):
                out_scr[l, pl.ds(b, per, stride=nph), :] = act[:, l * LANES:(l + 1) * LANES]
        xbc_ref[:, cs] = jnp.concatenate([out_scr[l] for l in range(COL_CHUNK // LANES)], axis=1).astype(BF16)

    n_z = wz_ref.shape[1]
    for c in range(n_z // COL_CHUNK):
        cs = slice(c * COL_CHUNK, (c + 1) * COL_CHUNK)
        z_ref[:, cs] = jnp.dot(hx_mb, wz_ref[:, cs], preferred_element_type=F32).astype(BF16)

    dt_raw = jnp.dot(hx_m, wdt_ref[...], precision=HIGHEST, preferred_element_type=F32) + dtb_ref[...]
    dt = jnp.maximum(dt_raw, 0.0) + jnp.log1p(jnp.exp(-jnp.abs(dt_raw)))
    dt_ref[...] = dt
    dtT_ref[...] = dt.T


def _mamba_inproj(X, mod2, w_in, conv_w, conv_b, dt_bias, nct_rows):
    n_rows, d_model = X.shape
    tm = ROW_TILE
    ntiles, nct = n_rows // tm, nct_rows // tm
    d_inner = w_in.shape[1] - conv_w.shape[1] - 2 * (dt_bias.shape[-1])
    n_xbc = conv_w.shape[1]
    n_dt = 2 * dt_bias.shape[-1]
    wz = w_in[:, :d_inner].astype(BF16)
    wxbc = w_in[:, d_inner:d_inner + n_xbc].astype(BF16)
    wdt = jnp.zeros((d_model, LANES), F32).at[:, :n_dt].set(w_in[:, d_inner + n_xbc:])
    dtb = jnp.zeros((1, LANES), F32).at[0, :n_dt].set(dt_bias.reshape(-1))
    cw = jnp.zeros((8, n_xbc), F32).at[:SSM_CONV].set(conv_w)
    prev, nxt = _halo_specs(d_model, tm, n_rows)
    kern = functools.partial(_mamba_inproj_kernel, tm=tm, nct=nct, ntiles=ntiles)
    return pl.pallas_call(
        kern,
        grid=(ntiles,),
        in_specs=[
            prev,
            pl.BlockSpec((tm, d_model), lambda i: (i, 0)),
            nxt,
            _mod_spec(d_model, nct),
            _resident(wz.shape), _resident(wxbc.shape), _resident(wdt.shape),
            _resident(cw.shape), _resident((1, n_xbc)), _resident(dtb.shape),
        ],
        out_specs=[
            pl.BlockSpec((tm, d_inner), lambda i: (i, 0)),
            pl.BlockSpec((tm, n_xbc), lambda i: (i, 0)),
            pl.BlockSpec((tm, LANES), lambda i: (i, 0)),
            pl.BlockSpec((LANES, tm), lambda i: (0, i)),
        ],
        out_shape=[
            jax.ShapeDtypeStruct((n_rows, d_inner), BF16),
            jax.ShapeDtypeStruct((n_rows, n_xbc), BF16),
            jax.ShapeDtypeStruct((n_rows, LANES), F32),
            jax.ShapeDtypeStruct((LANES, n_rows), F32),
        ],
        scratch_shapes=[pltpu.VMEM((d_model // LANES, tm, LANES), F32),
                        pltpu.VMEM((COL_CHUNK // LANES, tm, LANES), F32)],
        name="mamba_inproj",
    )(X, X, X, mod2, wz, wxbc, wdt, cw, conv_b.reshape(1, n_xbc), dtb)


def _split3(v):
    hi = v.astype(BF16)
    r1 = v - hi.astype(F32)
    mid = r1.astype(BF16)
    lo = (r1 - mid.astype(F32)).astype(BF16)
    return hi, mid, lo


def _ssd_direction(d, xs_ref, b_ref, c_ref, dt_ref, dtT_ref, arow_ref, acol_ref, r2_ref, y_ref,
                   state_scr, lower, upper, lane_head, *, L, n_heads):
    n_groups = SSM_GROUPS
    hpg = n_heads // n_groups
    gw = hpg * SSM_HEAD_DIM
    ns = SSM_STATE
    dt = dt_ref[...]
    dA = dt * arow_ref[0:1, :]
    dAT = dtT_ref[...] * acol_ref[...]
    low3, up3 = (jnp.concatenate([t.astype(BF16)] * 3, axis=1) for t in (lower, upper))
    low3v, up3v = (jnp.concatenate([t.astype(BF16)] * 3, axis=0) for t in (lower, upper))
    dA3 = jnp.concatenate(_split3(dA), axis=0)
    dAT3 = jnp.concatenate(_split3(dAT), axis=1)
    if d == 0:
        cum = jnp.dot(low3, dA3, preferred_element_type=F32)
        cumT = jnp.dot(dAT3, up3v, preferred_element_type=F32)
        tot = cum[L - 1:L]
        mask = lower
    else:
        cum = jnp.dot(up3, dA3, preferred_element_type=F32)
        cumT = jnp.dot(dAT3, low3v, preferred_element_type=F32)
        tot = cum[0:1]
        mask = upper
    e_in = jnp.exp(cum)
    tail = jnp.exp(tot - cum)
    st = jnp.concatenate([dt, dt * tail, e_in], axis=0)
    hi = st.astype(BF16)
    lo = (st - hi.astype(F32)).astype(BF16)
    lhs = jnp.concatenate([hi, lo], axis=1)
    neg_inf = jnp.float32(-jnp.inf)

    for g in range(n_groups):
        gs = slice(g * gw, (g + 1) * gw)
        rep = jnp.dot(lhs, r2_ref[:, gs], preferred_element_type=F32)
        xs_g = xs_ref[:, gs].astype(F32)
        xdt = xs_g * rep[0:L]
        xdt_tail = (xs_g * rep[L:2 * L]).astype(BF16)
        e_rep = rep[2 * L:3 * L]
        bg = b_ref[:, g * ns:(g + 1) * ns]
        cg = c_ref[:, g * ns:(g + 1) * ns]
        cb = lax.dot_general(cg, bg, (((1,), (1,)), ((), ())), preferred_element_type=F32)
        y = None
        for k in range(hpg):
            lane = d * n_heads + g * hpg + k
            seg = cum[:, lane:lane + 1] - cumT[lane:lane + 1, :]
            dec = jnp.exp(jnp.where(mask, seg, neg_inf))
            sc = (cb * dec).astype(BF16)
            xk = jnp.where(lane_head == k, xdt, 0.0).astype(BF16)
            yk = jnp.dot(sc, xk, preferred_element_type=F32)
            y = yk if y is None else y + yk
        h_t = state_scr[d, g]
        y_off = jnp.dot(cg, h_t.astype(BF16), preferred_element_type=F32) * e_rep
        e_tot = e_rep[L - 1:L] if d == 0 else e_rep[0:1]
        upd = lax.dot_general(bg, xdt_tail, (((0,), (0,)), ((), ())), preferred_element_type=F32)
        state_scr[d, g] = h_t * e_tot + upd
        y_ref[:, gs] = (y + y_off).astype(BF16)


def _ssd_kernel(xsf, bf, cf, dtf, dtTf, xsb, bb, cb_, dtb, dtTb, arow_ref, acol_ref, r2f_ref, r2b_ref,
                yf_ref, yb_ref, state_scr, *, L, n_heads):
    @pl.when(pl.program_id(0) == 0)
    def _():
        state_scr[...] = jnp.zeros_like(state_scr)

    r = lax.broadcasted_iota(jnp.int32, (L, L), 0)
    c = lax.broadcasted_iota(jnp.int32, (L, L), 1)
    lower, upper = c <= r, r <= c
    gw = (n_heads // SSM_GROUPS) * SSM_HEAD_DIM
    lane_head = lax.broadcasted_iota(jnp.int32, (1, gw), 1) // SSM_HEAD_DIM
    common = dict(L=L, n_heads=n_heads)
    _ssd_direction(0, xsf, bf, cf, dtf, dtTf, arow_ref, acol_ref, r2f_ref, yf_ref, state_scr,
                   lower, upper, lane_head, **common)
    _ssd_direction(1, xsb, bb, cb_, dtb, dtTb, arow_ref, acol_ref, r2b_ref, yb_ref, state_scr,
                   lower, upper, lane_head, **common)


def _ssd_scan(xbc, dt, dtT, a_log, d_inner):
    n_rows = xbc.shape[0]
    L = SSM_CHUNK
    n = n_rows // L
    n_heads = a_log.shape[-1]
    n_bc = SSM_GROUPS * SSM_STATE
    a = -jnp.exp(a_log.astype(F32)).reshape(-1)
    a_pad = jnp.zeros((LANES,), F32).at[:2 * n_heads].set(a)
    arow = jnp.zeros((8, LANES), F32).at[0].set(a_pad)
    acol = jnp.broadcast_to(a_pad[:, None], (LANES, L))
    head_of_col = jnp.arange(d_inner) // SSM_HEAD_DIM
    r2 = []
    for d in range(2):
        sel = (jnp.arange(LANES)[:, None] == (d * n_heads + head_of_col)[None, :]).astype(BF16)
        r2.append(jnp.concatenate([sel, sel], axis=0))
    fwd = lambda i: i
    bwd = lambda i: jnp.where(i == 0, 0, n - i)
    assert n_rows // L >= 2 and L == ROW_TILE

    def specs(cm):
        return [
            pl.BlockSpec((L, d_inner), lambda i: (cm(i), 0)),
            pl.BlockSpec((L, n_bc), lambda i: (cm(i), d_inner // n_bc)),
            pl.BlockSpec((L, n_bc), lambda i: (cm(i), d_inner // n_bc + 1)),
            pl.BlockSpec((L, LANES), lambda i: (cm(i), 0)),
            pl.BlockSpec((LANES, L), lambda i: (0, cm(i))),
        ]

    kern = functools.partial(_ssd_kernel, L=L, n_heads=n_heads)
    return pl.pallas_call(
        kern,
        grid=(n,),
        in_specs=specs(fwd) + specs(bwd) + [
            _resident(arow.shape), _resident(acol.shape), _resident(r2[0].shape), _resident(r2[1].shape)],
        out_specs=[
            pl.BlockSpec((L, d_inner), lambda i: (fwd(i), 0)),
            pl.BlockSpec((L, d_inner), lambda i: (bwd(i), 0)),
        ],
        out_shape=[jax.ShapeDtypeStruct((n_rows, d_inner), BF16)] * 2,
        scratch_shapes=[pltpu.VMEM((2, SSM_GROUPS, SSM_STATE, d_inner // SSM_GROUPS), F32)],
        compiler_params=pltpu.CompilerParams(dimension_semantics=("arbitrary",)),
        name="ssd_scan",
    )(xbc, xbc, xbc, dt, dtT, xbc, xbc, xbc, dt, dtT, arow, acol, r2[0], r2[1])


def _mamba_finish_kernel(yf_ref, yb_ref, xs_ref, z_ref, x_ref, mod_ref, drep_ref, ng_ref, wout_ref,
                         lng_ref, lnb_ref, o_ref):
    y = yf_ref[...].astype(F32) + yb_ref[...].astype(F32) + xs_ref[...].astype(F32) * drep_ref[...]
    t = y * _silu(z_ref[...].astype(F32))
    t = t * lax.rsqrt(jnp.mean(t * t, axis=-1, keepdims=True) + RMS_EPS) * ng_ref[...]
    out = jnp.dot(t.astype(BF16), wout_ref[...], preferred_element_type=F32)
    gate = mod_ref[0][2:3]
    o_ref[...] = _layer_norm(DEEPNORM_ALPHA * x_ref[...] + gate * out, lng_ref[...], lnb_ref[...])


def _mamba_finish(yf, yb, xbc, z, X, mod2, d_skip, norm_g, w_out, ln_g, ln_b, nct_rows, latent_only):
    n_rows, d_model = X.shape
    d_inner = z.shape[1]
    tm = ROW_TILE
    ntiles, nct = n_rows // tm, nct_rows // tm
    skip = nct if latent_only else 0
    drep = jnp.repeat(d_skip.astype(F32), SSM_HEAD_DIM).reshape(1, d_inner)
    row = lambda w: pl.BlockSpec((tm, w), lambda i: (i + skip, 0))
    return pl.pallas_call(
        _mamba_finish_kernel,
        grid=(ntiles - skip,),
        in_specs=[row(d_inner), row(d_inner), row(d_inner), row(d_inner), row(d_model),
                  _mod_spec(d_model, nct - skip), _resident((1, d_inner)), _resident((1, d_inner)),
                  _resident(w_out.shape), _resident((1, d_model)), _resident((1, d_model))],
        out_specs=pl.BlockSpec((tm, d_model), lambda i: (i, 0)),
        out_shape=jax.ShapeDtypeStruct((n_rows - skip * tm, d_model), F32),
        name="mamba_finish",
    )(yf, yb, xbc, z, X, mod2, drep, norm_g.reshape(1, d_inner), w_out.astype(BF16),
      ln_g.reshape(1, d_model), ln_b.reshape(1, d_model))


def _mamba_layer(X, mod2, w_in, conv_w, conv_b, dt_bias, a_log, d_skip, norm_g, w_out, ln_g, ln_b, nct_rows,
                 latent_only):
    z, xbc, dt, dtT = _mamba_inproj(X, mod2, w_in, conv_w, conv_b, dt_bias, nct_rows)
    yf, yb = _ssd_scan(xbc, dt, dtT, a_log, z.shape[1])
    return _mamba_finish(yf, yb, xbc, z, X, mod2, d_skip, norm_g, w_out, ln_g, ln_b, nct_rows, latent_only)


def _attn_inproj_kernel(x_ref, mod_ref, w_ref, gm_ref, qn_ref, kn_ref, cos_ref, sin_ref,
                        qT_ref, k_ref, vT_ref, g_ref, *, wq, wkv):
    m = mod_ref[0]
    hx = (x_ref[...] * (1.0 + m[1:2]) + m[0:1]).astype(BF16)
    hd = ATTN_HEAD_DIM

    def norm_rope(t, gmean, gain):
        width = t.shape[1]
        ms = jnp.dot((t * t).astype(BF16), gmean, preferred_element_type=F32)
        tn = t * lax.rsqrt(ms + RMS_EPS) * gain
        reps = width // LANES
        cs = jnp.concatenate([cos_ref[...]] * reps, axis=1)
        sn = jnp.concatenate([sin_ref[...]] * reps, axis=1)
        lane = lax.broadcasted_iota(jnp.int32, (1, width), 1)
        first_half = (lane % hd) < hd // 2
        swapped = jnp.where(first_half, pltpu.roll(tn, width - hd // 2, 1), pltpu.roll(tn, hd // 2, 1))
        return tn * cs + swapped * sn

    q = jnp.dot(hx, w_ref[:, 0:wq], preferred_element_type=F32)
    qr = norm_rope(q, gm_ref[...], qn_ref[...]) * (hd ** -0.5 * LOG2_E)
    qT_ref[...] = qr.T.astype(BF16)
    k = jnp.dot(hx, w_ref[:, wq:wq + wkv], preferred_element_type=F32)
    kr = norm_rope(k, gm_ref[0:wkv, 0:wkv], kn_ref[...])
    for j in range(wkv // hd):
        k_ref[j] = kr[:, j * hd:(j + 1) * hd].astype(BF16)
    v = jnp.dot(hx, w_ref[:, wq + wkv:wq + 2 * wkv], preferred_element_type=F32)
    v_t = v.T.astype(BF16)
    for j in range(wkv // hd):
        vT_ref[j, 0:hd, :] = v_t[j * hd:(j + 1) * hd, :]
        vT_ref[j, hd:, :] = jnp.ones((V_ROWS - hd, v_t.shape[1]), BF16)
    g_ref[...] = jnp.dot(hx, w_ref[:, wq + 2 * wkv:], preferred_element_type=F32).astype(BF16)


def _rope_tables(n_tok, n_ctx):
    t = jnp.arange(n_tok)
    row_ids = (t // GRID_W).astype(F32)
    col_ids = (t % GRID_W).astype(F32)
    half = ATTN_HEAD_DIM // 2
    inv = ROPE_THETA ** (-jnp.arange(0, half, 2, dtype=F32) / half)
    ang = jnp.concatenate([row_ids[:, None] * inv, col_ids[:, None] * inv], axis=-1)
    c, s = jnp.cos(ang), jnp.sin(ang)
    cos_h = jnp.concatenate([c, c], axis=-1)
    sin_h = jnp.concatenate([-s, s], axis=-1)
    reps = LANES // ATTN_HEAD_DIM
    cos2 = jnp.concatenate([jnp.ones((n_ctx, LANES), F32), jnp.tile(cos_h, (1, reps))], axis=0)
    sin2 = jnp.concatenate([jnp.zeros((n_ctx, LANES), F32), jnp.tile(sin_h, (1, reps))], axis=0)
    return cos2, sin2


def _attn_inproj(X, mod2, w_in, q_norm, k_norm, cos2, sin2, nct_rows):
    n_rows, d_model = X.shape
    tm = ROW_TILE
    ntiles, nct = n_rows // tm, nct_rows // tm
    hd = ATTN_HEAD_DIM
    wq, wkv = ATTN_HEADS * hd, ATTN_KV_HEADS * hd
    idx = jnp.arange(wq) // hd
    gmean = (idx[:, None] == idx[None, :]).astype(BF16) * (1.0 / hd)
    qn = jnp.tile(q_norm.astype(F32), ATTN_HEADS).reshape(1, wq)
    kn = jnp.tile(k_norm.astype(F32), ATTN_KV_HEADS).reshape(1, wkv)
    w = w_in.astype(BF16)
    kern = functools.partial(_attn_inproj_kernel, wq=wq, wkv=wkv)
    return pl.pallas_call(
        kern,
        grid=(ntiles,),
        in_specs=[
            pl.BlockSpec((tm, d_model), lambda i: (i, 0)),
            _mod_spec(d_model, nct),
            _resident(w.shape), _resident(gmean.shape), _resident(qn.shape), _resident(kn.shape),
            pl.BlockSpec((tm, LANES), lambda i: (i, 0)),
            pl.BlockSpec((tm, LANES), lambda i: (i, 0)),
        ],
        out_specs=[
            pl.BlockSpec((wq, tm), lambda i: (0, i)),
            pl.BlockSpec((ATTN_KV_HEADS, tm, hd), lambda i: (0, i, 0)),
            pl.BlockSpec((ATTN_KV_HEADS, V_ROWS, tm), lambda i: (0, 0, i)),
            pl.BlockSpec((tm, wq), lambda i: (i, 0)),
        ],
        out_shape=[
            jax.ShapeDtypeStruct((wq, n_rows), BF16),
            jax.ShapeDtypeStruct((ATTN_KV_HEADS, n_rows, hd), BF16),
            jax.ShapeDtypeStruct((ATTN_KV_HEADS, V_ROWS, n_rows), BF16),
            jax.ShapeDtypeStruct((n_rows, wq), BF16),
        ],
        name="attn_inproj",
    )(X, mod2, w, gmean, qn, kn, cos2, sin2)


def _flash_kernel(qT_ref, k_ref, vT_ref, o_ref, s_scr, m_scr, acc_scr, *, bk, nct_tiles, nct_chunks, n_chunks,
                  bounded):
    qi = pl.program_id(1)
    hd = ATTN_HEAD_DIM
    n_rep = ATTN_HEADS // ATTN_KV_HEADS
    m_scr[...] = jnp.full_like(m_scr, -jnp.inf)
    acc_scr[...] = jnp.zeros_like(acc_scr)
    nch = jnp.where(qi < nct_tiles, nct_chunks, n_chunks)

    def produce(buf, j):
        off = pl.multiple_of(jnp.minimum(j, nch - 1) * bk, bk)
        kb = k_ref[0, pl.ds(off, bk), :]
        for g in range(n_rep):
            s_scr[buf, g] = jnp.dot(kb, qT_ref[g * hd:(g + 1) * hd, :], preferred_element_type=F32)

    def consume(buf, j):
        off = pl.multiple_of(j * bk, bk)
        vb = vT_ref[0, :, pl.ds(off, bk)]
        for g in range(n_rep):
            s = s_scr[buf, g]
            if bounded:
                acc_scr[g] += jnp.dot(vb, jnp.exp2(s).astype(BF16), preferred_element_type=F32)
                continue
            m_old = m_scr[g]
            m_new = jnp.maximum(m_old, jnp.max(s, axis=0, keepdims=True))
            alpha = jnp.exp2(m_old - m_new)
            p = jnp.exp2(s - m_new).astype(BF16)
            acc_scr[g] = alpha * acc_scr[g] + jnp.dot(vb, p, preferred_element_type=F32)
            m_scr[g] = m_new

    produce(0, 0)
    unroll = FLASH_UNROLL_BOUNDED if bounded else FLASH_UNROLL
    n_trips = nch // unroll

    def trip(t, carry):
        for u in range(unroll):
            j = unroll * t + u
            produce((u + 1) % 2, j + 1)
            consume(u % 2, j)
        return carry

    lax.fori_loop(0, n_trips, trip, 0)
    done = n_trips * unroll

    @pl.when(done < nch)
    def _():
        consume(0, done)

    def leftover(j, carry):
        produce(0, j)
        consume(0, j)
        return carry

    lax.fori_loop(done + 1, nch, leftover, 0)

    o_t = jnp.concatenate([acc_scr[g, 0:hd] / acc_scr[g, hd:hd + 1] for g in range(n_rep)], axis=0)
    o_ref[...] = o_t.T.astype(BF16)


def _flash_attention(qT, k, vT, nct_rows, scores_bounded):
    wq, n_rows = qT.shape
    hd = ATTN_HEAD_DIM
    n_rep = ATTN_HEADS // ATTN_KV_HEADS
    bq = bk = ROW_TILE
    static = dict(bk=bk, nct_tiles=nct_rows // bq, nct_chunks=nct_rows // bk, n_chunks=n_rows // bk)

    def call(bounded, name):
        return pl.pallas_call(
            functools.partial(_flash_kernel, bounded=bounded, **static),
            grid=(ATTN_KV_HEADS, n_rows // bq),
            in_specs=[
                pl.BlockSpec((n_rep * hd, bq), lambda h, i: (h, i)),
                pl.BlockSpec((1, n_rows, hd), lambda h, i: (h, 0, 0)),
                pl.BlockSpec((1, V_ROWS, n_rows), lambda h, i: (h, 0, 0)),
            ],
            out_specs=pl.BlockSpec((bq, n_rep * hd), lambda h, i: (i, h)),
            out_shape=jax.ShapeDtypeStruct((n_rows, wq), BF16),
            scratch_shapes=[
                pltpu.VMEM((2, n_rep, bk, bq), F32),
                pltpu.VMEM((n_rep, 1, bq), F32),
                pltpu.VMEM((n_rep, V_ROWS, bq), F32),
            ],
            name=name,
        )

    return lax.cond(scores_bounded, call(True, "flash_attention_bounded"), call(False, "flash_attention"),
                    qT, k, vT)


def _attn_out_kernel(o_ref, g_ref, x_ref, mod_ref, wout_ref, lng_ref, lnb_ref, out_ref):
    t = o_ref[...].astype(F32) * _silu(g_ref[...].astype(F32))
    out = jnp.dot(t.astype(BF16), wout_ref[...], preferred_element_type=F32)
    gate = mod_ref[0][2:3]
    out_ref[...] = _layer_norm(DEEPNORM_ALPHA * x_ref[...] + gate * out, lng_ref[...], lnb_ref[...])


def _attn_out(o, g, X, mod2, w_out, ln_g, ln_b, nct_rows):
    n_rows, d_model = X.shape
    wq = o.shape[1]
    tm = ROW_TILE
    ntiles, nct = n_rows // tm, nct_rows // tm
    row = lambda w: pl.BlockSpec((tm, w), lambda i: (i, 0))
    return pl.pallas_call(
        _attn_out_kernel,
        grid=(ntiles,),
        in_specs=[row(wq), row(wq), row(d_model), _mod_spec(d_model, nct), _resident(w_out.shape),
                  _resident((1, d_model)), _resident((1, d_model))],
        out_specs=row(d_model),
        out_shape=jax.ShapeDtypeStruct((n_rows, d_model), F32),
        name="attn_out",
    )(o, g, X, mod2, w_out.astype(BF16), ln_g.reshape(1, d_model), ln_b.reshape(1, d_model))


def _attn_layer(X, mod2, w_in, q_norm, k_norm, w_out, ln_g, ln_b, cos2, sin2, nct_rows):
    qT, k, vT, g = _attn_inproj(X, mod2, w_in, q_norm, k_norm, cos2, sin2, nct_rows)
    score_bound = LOG2_E * ATTN_HEAD_DIM ** 0.5 * jnp.max(jnp.abs(q_norm)) * jnp.max(jnp.abs(k_norm))
    o = _flash_attention(qT, k, vT, nct_rows, score_bound <= SAFE_LOG2_SCORE)
    return _attn_out(o, g, X, mod2, w_out, ln_g, ln_b, nct_rows)


def _pool_kernel(xp_ref, xm_ref, xn_ref, mod_ref, win_ref, gw_ref, ls_ref, wout_ref, lng_ref, lnb_ref,
                 o_ref, ext_scr, t_scr, *, tm, nct, ntiles, n_ctx, n_tok):
    i = pl.program_id(0)
    m = mod_ref[0]
    shift, scale, gate = m[0:1], m[1:2], m[2:3]

    def modulate(v):
        return v * (1.0 + scale) + shift

    x_m = xm_ref[...]
    hx_m = modulate(x_m)
    hx_ext = jnp.concatenate([modulate(xp_ref[...]), hx_m, modulate(xn_ref[...])], axis=0).astype(BF16)
    hx_mb = hx_m.astype(BF16)
    maskf = _halo_row_mask(i, tm, nct, ntiles)

    is_ctx = i < nct
    t0 = (i - jnp.where(is_ctx, 0, nct)) * tm
    n_seq = jnp.where(is_ctx, n_ctx, n_tok)
    pos = t0 + lax.broadcasted_iota(jnp.int32, (tm, 1), 0)

    n_groups = len(POOL_WINDOWS)
    gd = gw_ref.shape[1]
    width = n_groups * gd
    for gi, window in enumerate(POOL_WINDOWS):
        cs = slice(gi * gd, (gi + 1) * gd)
        ext_scr[...] = jnp.dot(hx_ext, win_ref[:, cs], preferred_element_type=F32) * maskf
        lo_off = window // 2
        hi_off = window - 1 - lo_off
        win_sum = ext_scr[pl.ds(HALO - lo_off, tm), :]
        for o in range(-lo_off + 1, hi_off + 1):
            win_sum = win_sum + ext_scr[pl.ds(HALO + o, tm), :]
        cnt = (jnp.minimum(pos + hi_off + 1, n_seq) - jnp.maximum(pos - lo_off, 0)).astype(F32)
        pooled = win_sum / cnt - ext_scr[pl.ds(HALO, tm), :]
        mixed = jnp.dot(pooled.astype(BF16), gw_ref[gi], preferred_element_type=F32)
        zc = jnp.dot(hx_mb, win_ref[:, width + gi * gd:width + (gi + 1) * gd], preferred_element_type=F32)
        t_scr[:, cs] = ((mixed * ls_ref[:, cs]) * _silu(zc)).astype(BF16)
    out = jnp.dot(t_scr[...], wout_ref[...], preferred_element_type=F32)
    o_ref[...] = _layer_norm(DEEPNORM_ALPHA * x_m + gate * out, lng_ref[...], lnb_ref[...])


def _pool_layer(X, mod2, w_in, group_w, layer_scale, w_out, ln_g, ln_b, nct_rows):
    n_rows, d_model = X.shape
    tm = ROW_TILE
    ntiles, nct = n_rows // tm, nct_rows // tm
    width = w_out.shape[0]
    gd = group_w.shape[1]
    assert max(POOL_WINDOWS) // 2 <= HALO
    prev, nxt = _halo_specs(d_model, tm, n_rows)
    kern = functools.partial(_pool_kernel, tm=tm, nct=nct, ntiles=ntiles, n_ctx=nct_rows,
                             n_tok=n_rows - nct_rows)
    return pl.pallas_call(
        kern,
        grid=(ntiles,),
        in_specs=[
            prev,
            pl.BlockSpec((tm, d_model), lambda i: (i, 0)),
            nxt,
            _mod_spec(d_model, nct),
            _resident(w_in.shape), _resident(group_w.shape), _resident((1, width)), _resident(w_out.shape),
            _resident((1, d_model)), _resident((1, d_model)),
        ],
        out_specs=pl.BlockSpec((tm, d_model), lambda i: (i, 0)),
        out_shape=jax.ShapeDtypeStruct((n_rows, d_model), F32),
        scratch_shapes=[pltpu.VMEM((tm + 2 * HALO, gd), F32), pltpu.VMEM((tm, width), BF16)],
        name="pool_mixer",
    )(X, X, X, mod2, w_in.astype(BF16), group_w.astype(BF16), layer_scale.reshape(1, width),
      w_out.astype(BF16), ln_g.reshape(1, d_model), ln_b.reshape(1, d_model))


def kernel(x, c, ctx, c_ctx, mod_w, mod_b, ln_g, ln_b, ssm_w_in, ssm_conv_w, ssm_conv_b, ssm_dt_bias,
           ssm_a_log, ssm_d, ssm_norm_g, ssm_w_out, attn_w_in, attn_q_norm, attn_k_norm, attn_w_out,
           pool_w_in, pool_group_w, pool_scale, pool_w_out):
    bsz, n_tok, d_model = x.shape
    n_ctx = ctx.shape[1]
    assert bsz == 1 and n_ctx % ROW_TILE == 0 and n_tok % ROW_TILE == 0
    X = jnp.concatenate([ctx[0], x[0]], axis=0)
    mods = _mod_vectors(c, c_ctx, mod_w, mod_b)
    cos2, sin2 = _rope_tables(n_tok, n_ctx)
    for i in range(DEPTH):
        kind, j = i % N_MIXERS, i // N_MIXERS
        m3 = mods[i, 0:2].reshape(2, 3, d_model)
        mod2 = jnp.zeros((2, 8, d_model), F32).at[:, 0:3].set(m3)
        if kind == 0:
            X = _mamba_layer(X, mod2, ssm_w_in[j], ssm_conv_w[j], ssm_conv_b[j], ssm_dt_bias[j],
                             ssm_a_log[j], ssm_d[j], ssm_norm_g[j], ssm_w_out[j], ln_g[i], ln_b[i], n_ctx,
                             latent_only=(i == DEPTH - 1))
        elif kind == 1:
            X = _attn_layer(X, mod2, attn_w_in[j], attn_q_norm[j], attn_k_norm[j], attn_w_out[j],
                            ln_g[i], ln_b[i], cos2, sin2, n_ctx)
        else:
            X = _pool_layer(X, mod2, pool_w_in[j], pool_group_w[j], pool_scale[j], pool_w_out[j],
                            ln_g[i], ln_b[i], n_ctx)
    if (DEPTH - 1) % N_MIXERS != 0:
        X = X[n_ctx:]
    return X[None]
```

```python
import functools

import jax
import jax.numpy as jnp
from jax import lax
from jax.experimental import pallas as pl
from jax.experimental.pallas import tpu as pltpu

F32 = jnp.float32
BF16 = jnp.bfloat16
HIGHEST = lax.Precision.HIGHEST

DEPTH = 4
N_MIXERS = 3
GRID_W = 64
ROPE_THETA = 10000.0

SSM_HEAD_DIM = 64
SSM_STATE = 128
SSM_GROUPS = 8
SSM_CONV = 5
SSM_CHUNK = 256

ATTN_HEADS = 16
ATTN_KV_HEADS = 4
ATTN_HEAD_DIM = 64
V_ROWS = ATTN_HEAD_DIM + 16
LOG2_E = 1.4426950408889634
SAFE_LOG2_SCORE = 60.0
FLASH_UNROLL = 8
FLASH_UNROLL_BOUNDED = 32

POOL_WINDOWS = (2, 4, 8, 16)

DEEPNORM_ALPHA = (2 * DEPTH) ** 0.25
LN_EPS = 1e-5
RMS_EPS = 1e-6

ROW_TILE = 256
HALO = 8
LANES = 128
COL_CHUNK = 1024


def _silu(v):
    return v * jax.nn.sigmoid(v)


def _layer_norm(r, g, b):
    mu = jnp.mean(r, axis=-1, keepdims=True)
    d = r - mu
    var = jnp.mean(d * d, axis=-1, keepdims=True)
    return d * lax.rsqrt(var + LN_EPS) * g + b


def _resident(shape):
    nd = len(shape)
    return pl.BlockSpec(shape, lambda *_: (0,) * nd, pipeline_mode=pl.Buffered(1))


def _mod_spec(d_model, nct):
    return pl.BlockSpec((1, 8, d_model), lambda i: (jnp.where(i < nct, 1, 0), 0, 0))


def _halo_specs(d_model, tm, n_rows):
    per = tm // HALO
    last = n_rows // HALO - 1
    prev = pl.BlockSpec((HALO, d_model), lambda i: (jnp.maximum(i * per - 1, 0), 0))
    nxt = pl.BlockSpec((HALO, d_model), lambda i: (jnp.minimum((i + 1) * per, last), 0))
    return prev, nxt


def _halo_ok(i, nct, ntiles):
    prev_ok = jnp.logical_and(i != 0, i != nct)
    next_ok = jnp.logical_and(i != nct - 1, i != ntiles - 1)
    return prev_ok, next_ok


def _halo_row_mask(i, tm, nct, ntiles):
    prev_ok, next_ok = _halo_ok(i, nct, ntiles)
    lo = jnp.where(prev_ok, 0, HALO)
    hi = jnp.where(next_ok, tm + 2 * HALO, tm + HALO)
    rows = lax.broadcasted_iota(jnp.int32, (tm + 2 * HALO, 1), 0)
    return jnp.logical_and(rows >= lo, rows < hi).astype(F32)


def _mod_kernel(cT_ref, w_ref, b_ref, o_ref):
    s = _silu(cT_ref[...])
    w = w_ref[0]
    rows = [jnp.sum(w * s[:, r:r + 1], axis=0, keepdims=True) for r in range(2)]
    pad = jnp.zeros((o_ref.shape[1] - 2, w.shape[1]), F32)
    o_ref[0] = jnp.concatenate(rows + [pad], axis=0) + b_ref[0]


def _mod_vectors(c, c_ctx, mod_w, mod_b):
    depth, d_model, d3 = mod_w.shape
    cpad = jnp.zeros((d_model, 8), F32).at[:, 0].set(c[0]).at[:, 1].set(c_ctx)
    out = pl.pallas_call(
        _mod_kernel,
        grid=(depth, d3 // d_model),
        in_specs=[
            pl.BlockSpec((d_model, 8), lambda i, j: (0, 0)),
            pl.BlockSpec((1, d_model, d_model), lambda i, j: (i, 0, j)),
            pl.BlockSpec((1, 1, d_model), lambda i, j: (i, 0, j)),
        ],
        out_specs=pl.BlockSpec((1, 8, d_model), lambda i, j: (i, 0, j)),
        out_shape=jax.ShapeDtypeStruct((depth, 8, d3), F32),
        name="mod_vectors",
    )(cpad, mod_w, mod_b.reshape(depth, 1, d3))
    return out


def _mamba_inproj_kernel(xp_ref, xm_ref, xn_ref, mod_ref, wz_ref, wxbc_ref, wdt_ref, cw_ref, cb_ref,
                         dtb_ref, z_ref, xbc_ref, dt_ref, dtT_ref, xs_scr, out_scr, *, tm, nct, ntiles):
    i = pl.program_id(0)
    m = mod_ref[0]
    shift, scale = m[0:1], m[1:2]

    def modulate(v):
        return v * (1.0 + scale) + shift

    hx_m = modulate(xm_ref[...])
    hx_mb = hx_m.astype(BF16)
    pad = SSM_CONV // 2

    nph = HALO
    per = tm // nph
    blk = per + HALO
    prev_ok, next_ok = _halo_ok(i, nct, ntiles)
    lead_row = lax.broadcasted_iota(jnp.int32, (HALO, 1), 0)
    next_row = jnp.where(next_ok, 0, -1)
    prev_row = jnp.where(prev_ok, HALO - 1, -1)
    n_slab = xm_ref.shape[1] // LANES
    for l in range(n_slab):
        xs_scr[l] = hx_m[:, l * LANES:(l + 1) * LANES]
    pieces = []
    for b in range(nph):
        lead = jnp.where(lead_row == next_row, modulate(xn_ref[b:b + 1, :]),
                         jnp.where(lead_row == prev_row, modulate(xp_ref[b:b + 1, :]), 0.0))
        body = jnp.concatenate([xs_scr[l, pl.ds(b, per, stride=nph), :] for l in range(n_slab)], axis=1)
        pieces += [lead, body]
    hx_perm = jnp.concatenate(pieces, axis=0).astype(BF16)

    n_xbc = wxbc_ref.shape[1]
    for c in range(n_xbc // COL_CHUNK):
        cs = slice(c * COL_CHUNK, (c + 1) * COL_CHUNK)
        pre = jnp.dot(hx_perm, wxbc_ref[:, cs], preferred_element_type=F32)
        blocks = {b: pre[b * blk:(b + 1) * blk] for b in range(nph)}
        for b in range(pad):
            blocks[b + nph] = pltpu.roll(blocks[b], blk - 1, 0)
            blocks[-1 - b] = pltpu.roll(blocks[nph - 1 - b], 1, 0)
        for b in range(nph):
            acc = cb_ref[:, cs] + cw_ref[0:1, cs] * blocks[b - pad][HALO:]
            for k in range(1, SSM_CONV):
                acc = acc + cw_ref[k:k + 1, cs] * blocks[b + k - pad][HALO:]
            act = _silu(acc)
            for l in range(COL_CHUNK // LANES):
                out_scr[c % 2, l, pl.ds(b, per, stride=nph), :] = act[:, l * LANES:(l + 1) * LANES]
        xbc_ref[:, cs] = jnp.concatenate([out_scr[c % 2, l] for l in range(COL_CHUNK // LANES)],
                                         axis=1).astype(BF16)

    n_z = wz_ref.shape[1]
    for c in range(n_z // COL_CHUNK):
        cs = slice(c * COL_CHUNK, (c + 1) * COL_CHUNK)
        z_ref[:, cs] = jnp.dot(hx_mb, wz_ref[:, cs], preferred_element_type=F32).astype(BF16)

    dt_raw = jnp.dot(hx_m, wdt_ref[...], precision=HIGHEST, preferred_element_type=F32) + dtb_ref[...]
    dt = jnp.maximum(dt_raw, 0.0) + jnp.log1p(jnp.exp(-jnp.abs(dt_raw)))
    dt_ref[...] = dt
    dtT_ref[...] = dt.T


def _mamba_inproj(X, mod2, w_in, conv_w, conv_b, dt_bias, nct_rows):
    n_rows, d_model = X.shape
    tm = ROW_TILE
    ntiles, nct = n_rows // tm, nct_rows // tm
    d_inner = w_in.shape[1] - conv_w.shape[1] - 2 * (dt_bias.shape[-1])
    n_xbc = conv_w.shape[1]
    n_dt = 2 * dt_bias.shape[-1]
    wz = w_in[:, :d_inner].astype(BF16)
    wxbc = w_in[:, d_inner:d_inner + n_xbc].astype(BF16)
    wdt = jnp.zeros((d_model, LANES), F32).at[:, :n_dt].set(w_in[:, d_inner + n_xbc:])
    dtb = jnp.zeros((1, LANES), F32).at[0, :n_dt].set(dt_bias.reshape(-1))
    cw = jnp.zeros((8, n_xbc), F32).at[:SSM_CONV].set(conv_w)
    prev, nxt = _halo_specs(d_model, tm, n_rows)
    kern = functools.partial(_mamba_inproj_kernel, tm=tm, nct=nct, ntiles=ntiles)
    return pl.pallas_call(
        kern,
        grid=(ntiles,),
        in_specs=[
            prev,
            pl.BlockSpec((tm, d_model), lambda i: (i, 0)),
            nxt,
            _mod_spec(d_model, nct),
            _resident(wz.shape), _resident(wxbc.shape), _resident(wdt.shape),
            _resident(cw.shape), _resident((1, n_xbc)), _resident(dtb.shape),
        ],
        out_specs=[
            pl.BlockSpec((tm, d_inner), lambda i: (i, 0)),
            pl.BlockSpec((tm, n_xbc), lambda i: (i, 0)),
            pl.BlockSpec((tm, LANES), lambda i: (i, 0)),
            pl.BlockSpec((LANES, tm), lambda i: (0, i)),
        ],
        out_shape=[
            jax.ShapeDtypeStruct((n_rows, d_inner), BF16),
            jax.ShapeDtypeStruct((n_rows, n_xbc), BF16),
            jax.ShapeDtypeStruct((n_rows, LANES), F32),
            jax.ShapeDtypeStruct((LANES, n_rows), F32),
        ],
        scratch_shapes=[pltpu.VMEM((d_model // LANES, tm, LANES), F32),
                        pltpu.VMEM((2, COL_CHUNK // LANES, tm, LANES), F32)],
        name="mamba_inproj",
    )(X, X, X, mod2, wz, wxbc, wdt, cw, conv_b.reshape(1, n_xbc), dtb)


def _split3(v):
    hi = v.astype(BF16)
    r1 = v - hi.astype(F32)
    mid = r1.astype(BF16)
    lo = (r1 - mid.astype(F32)).astype(BF16)
    return hi, mid, lo


def _ssd_direction(d, xs_ref, b_ref, c_ref, dt_ref, dtT_ref, arow_ref, acol_ref, r2_ref, y_ref,
                   state_scr, lower, upper, lane_head, *, L, n_heads):
    n_groups = SSM_GROUPS
    hpg = n_heads // n_groups
    gw = hpg * SSM_HEAD_DIM
    ns = SSM_STATE
    dt = dt_ref[...]
    dA = dt * arow_ref[0:1, :]
    dAT = dtT_ref[...] * acol_ref[...]
    low3, up3 = (jnp.concatenate([t.astype(BF16)] * 3, axis=1) for t in (lower, upper))
    low3v, up3v = (jnp.concatenate([t.astype(BF16)] * 3, axis=0) for t in (lower, upper))
    dA3 = jnp.concatenate(_split3(dA), axis=0)
    dAT3 = jnp.concatenate(_split3(dAT), axis=1)
    if d == 0:
        cum = jnp.dot(low3, dA3, preferred_element_type=F32)
        cumT = jnp.dot(dAT3, up3v, preferred_element_type=F32)
        tot = cum[L - 1:L]
        mask = lower
    else:
        cum = jnp.dot(up3, dA3, preferred_element_type=F32)
        cumT = jnp.dot(dAT3, low3v, preferred_element_type=F32)
        tot = cum[0:1]
        mask = upper
    e_in = jnp.exp(cum)
    tail = jnp.exp(tot - cum)
    st = jnp.concatenate([dt * tail, e_in], axis=0)
    hi = st.astype(BF16)
    lo = (st - hi.astype(F32)).astype(BF16)
    lhs = jnp.concatenate([hi, lo], axis=1)
    dtT = dtT_ref[...]
    neg_inf = jnp.float32(-jnp.inf)

    for g in range(n_groups):
        gs = slice(g * gw, (g + 1) * gw)
        rep = jnp.dot(lhs, r2_ref[:, gs], preferred_element_type=F32)
        xs_b = xs_ref[:, gs]
        xdt_tail = (xs_b.astype(F32) * rep[0:L]).astype(BF16)
        e_rep = rep[L:2 * L]
        bg = b_ref[:, g * ns:(g + 1) * ns]
        cg = c_ref[:, g * ns:(g + 1) * ns]
        cb = lax.dot_general(cg, bg, (((1,), (1,)), ((), ())), preferred_element_type=F32)
        y = None
        for k in range(hpg):
            lane = d * n_heads + g * hpg + k
            seg = cum[:, lane:lane + 1] - cumT[lane:lane + 1, :]
            dec = jnp.exp(jnp.where(mask, seg, neg_inf))
            sc = (cb * dec * dtT[lane:lane + 1, :]).astype(BF16)
            xk = jnp.where(lane_head == k, xs_b, jnp.zeros_like(xs_b))
            yk = jnp.dot(sc, xk, preferred_element_type=F32)
            y = yk if y is None else y + yk
        h_t = state_scr[d, g]
        y_off = jnp.dot(cg, h_t.astype(BF16), preferred_element_type=F32) * e_rep
        e_tot = e_rep[L - 1:L] if d == 0 else e_rep[0:1]
        upd = lax.dot_general(bg, xdt_tail, (((0,), (0,)), ((), ())), preferred_element_type=F32)
        state_scr[d, g] = h_t * e_tot + upd
        y_ref[:, gs] = (y + y_off).astype(BF16)


def _ssd_kernel(xsf, bf, cf, dtf, dtTf, xsb, bb, cb_, dtb, dtTb, arow_ref, acol_ref, r2f_ref, r2b_ref,
                yf_ref, yb_ref, state_scr, *, L, n_heads):
    @pl.when(pl.program_id(0) == 0)
    def _():
        state_scr[...] = jnp.zeros_like(state_scr)

    r = lax.broadcasted_iota(jnp.int32, (L, L), 0)
    c = lax.broadcasted_iota(jnp.int32, (L, L), 1)
    lower, upper = c <= r, r <= c
    gw = (n_heads // SSM_GROUPS) * SSM_HEAD_DIM
    lane_head = lax.broadcasted_iota(jnp.int32, (1, gw), 1) // SSM_HEAD_DIM
    common = dict(L=L, n_heads=n_heads)
    _ssd_direction(0, xsf, bf, cf, dtf, dtTf, arow_ref, acol_ref, r2f_ref, yf_ref, state_scr,
                   lower, upper, lane_head, **common)
    _ssd_direction(1, xsb, bb, cb_, dtb, dtTb, arow_ref, acol_ref, r2b_ref, yb_ref, state_scr,
                   lower, upper, lane_head, **common)


def _ssd_scan(xbc, dt, dtT, a_log, d_inner):
    n_rows = xbc.shape[0]
    L = SSM_CHUNK
    n = n_rows // L
    n_heads = a_log.shape[-1]
    n_bc = SSM_GROUPS * SSM_STATE
    a = -jnp.exp(a_log.astype(F32)).reshape(-1)
    a_pad = jnp.zeros((LANES,), F32).at[:2 * n_heads].set(a)
    arow = jnp.zeros((8, LANES), F32).at[0].set(a_pad)
    acol = jnp.broadcast_to(a_pad[:, None], (LANES, L))
    head_of_col = jnp.arange(d_inner) // SSM_HEAD_DIM
    r2 = []
    for d in range(2):
        sel = (jnp.arange(LANES)[:, None] == (d * n_heads + head_of_col)[None, :]).astype(BF16)
        r2.append(jnp.concatenate([sel, sel], axis=0))
    fwd = lambda i: i
    bwd = lambda i: jnp.where(i == 0, 0, n - i)
    assert n_rows // L >= 2 and L == ROW_TILE

    def specs(cm):
        return [
            pl.BlockSpec((L, d_inner), lambda i: (cm(i), 0)),
            pl.BlockSpec((L, n_bc), lambda i: (cm(i), d_inner // n_bc)),
            pl.BlockSpec((L, n_bc), lambda i: (cm(i), d_inner // n_bc + 1)),
            pl.BlockSpec((L, LANES), lambda i: (cm(i), 0)),
            pl.BlockSpec((LANES, L), lambda i: (0, cm(i))),
        ]

    kern = functools.partial(_ssd_kernel, L=L, n_heads=n_heads)
    return pl.pallas_call(
        kern,
        grid=(n,),
        in_specs=specs(fwd) + specs(bwd) + [
            _resident(arow.shape), _resident(acol.shape), _resident(r2[0].shape), _resident(r2[1].shape)],
        out_specs=[
            pl.BlockSpec((L, d_inner), lambda i: (fwd(i), 0)),
            pl.BlockSpec((L, d_inner), lambda i: (bwd(i), 0)),
        ],
        out_shape=[jax.ShapeDtypeStruct((n_rows, d_inner), BF16)] * 2,
        scratch_shapes=[pltpu.VMEM((2, SSM_GROUPS, SSM_STATE, d_inner // SSM_GROUPS), F32)],
        compiler_params=pltpu.CompilerParams(dimension_semantics=("arbitrary",)),
        name="ssd_scan",
    )(xbc, xbc, xbc, dt, dtT, xbc, xbc, xbc, dt, dtT, arow, acol, r2[0], r2[1])


def _mamba_finish_kernel(yf_ref, yb_ref, xs_ref, z_ref, x_ref, mod_ref, drep_ref, ng_ref, wout_ref,
                         lng_ref, lnb_ref, o_ref):
    y = yf_ref[...].astype(F32) + yb_ref[...].astype(F32) + xs_ref[...].astype(F32) * drep_ref[...]
    t = y * _silu(z_ref[...].astype(F32))
    t = t * lax.rsqrt(jnp.mean(t * t, axis=-1, keepdims=True) + RMS_EPS) * ng_ref[...]
    out = jnp.dot(t.astype(BF16), wout_ref[...], preferred_element_type=F32)
    gate = mod_ref[0][2:3]
    o_ref[...] = _layer_norm(DEEPNORM_ALPHA * x_ref[...] + gate * out, lng_ref[...], lnb_ref[...])


def _mamba_finish(yf, yb, xbc, z, X, mod2, d_skip, norm_g, w_out, ln_g, ln_b, nct_rows, latent_only):
    n_rows, d_model = X.shape
    d_inner = z.shape[1]
    tm = ROW_TILE
    ntiles, nct = n_rows // tm, nct_rows // tm
    skip = nct if latent_only else 0
    drep = jnp.repeat(d_skip.astype(F32), SSM_HEAD_DIM).reshape(1, d_inner)
    row = lambda w: pl.BlockSpec((tm, w), lambda i: (i + skip, 0))
    return pl.pallas_call(
        _mamba_finish_kernel,
        grid=(ntiles - skip,),
        in_specs=[row(d_inner), row(d_inner), row(d_inner), row(d_inner), row(d_model),
                  _mod_spec(d_model, nct - skip), _resident((1, d_inner)), _resident((1, d_inner)),
                  _resident(w_out.shape), _resident((1, d_model)), _resident((1, d_model))],
        out_specs=pl.BlockSpec((tm, d_model), lambda i: (i, 0)),
        out_shape=jax.ShapeDtypeStruct((n_rows - skip * tm, d_model), F32),
        name="mamba_finish",
    )(yf, yb, xbc, z, X, mod2, drep, norm_g.reshape(1, d_inner), w_out.astype(BF16),
      ln_g.reshape(1, d_model), ln_b.reshape(1, d_model))


def _mamba_layer(X, mod2, w_in, conv_w, conv_b, dt_bias, a_log, d_skip, norm_g, w_out, ln_g, ln_b, nct_rows,
                 latent_only):
    z, xbc, dt, dtT = _mamba_inproj(X, mod2, w_in, conv_w, conv_b, dt_bias, nct_rows)
    yf, yb = _ssd_scan(xbc, dt, dtT, a_log, z.shape[1])
    return _mamba_finish(yf, yb, xbc, z, X, mod2, d_skip, norm_g, w_out, ln_g, ln_b, nct_rows, latent_only)


def _attn_inproj_kernel(x_ref, mod_ref, w_ref, gm_ref, qn_ref, kn_ref, cos_ref, sin_ref,
                        qT_ref, k_ref, vT_ref, g_ref, *, wq, wkv):
    m = mod_ref[0]
    hx = (x_ref[...] * (1.0 + m[1:2]) + m[0:1]).astype(BF16)
    hd = ATTN_HEAD_DIM

    def norm_rope(t, gmean, gain):
        width = t.shape[1]
        ms = jnp.dot((t * t).astype(BF16), gmean, preferred_element_type=F32)
        tn = t * lax.rsqrt(ms + RMS_EPS) * gain
        reps = width // LANES
        cs = jnp.concatenate([cos_ref[...]] * reps, axis=1)
        sn = jnp.concatenate([sin_ref[...]] * reps, axis=1)
        lane = lax.broadcasted_iota(jnp.int32, (1, width), 1)
        first_half = (lane % hd) < hd // 2
        swapped = jnp.where(first_half, pltpu.roll(tn, width - hd // 2, 1), pltpu.roll(tn, hd // 2, 1))
        return tn * cs + swapped * sn

    q = jnp.dot(hx, w_ref[:, 0:wq], preferred_element_type=F32)
    qr = norm_rope(q, gm_ref[...], qn_ref[...]) * (hd ** -0.5 * LOG2_E)
    qT_ref[...] = qr.T.astype(BF16)
    k = jnp.dot(hx, w_ref[:, wq:wq + wkv], preferred_element_type=F32)
    kr = norm_rope(k, gm_ref[0:wkv, 0:wkv], kn_ref[...])
    for j in range(wkv // hd):
        k_ref[j] = kr[:, j * hd:(j + 1) * hd].astype(BF16)
    v = jnp.dot(hx, w_ref[:, wq + wkv:wq + 2 * wkv], preferred_element_type=F32)
    v_t = v.T.astype(BF16)
    for j in range(wkv // hd):
        vT_ref[j, 0:hd, :] = v_t[j * hd:(j + 1) * hd, :]
        vT_ref[j, hd:, :] = jnp.ones((V_ROWS - hd, v_t.shape[1]), BF16)
    g_ref[...] = jnp.dot(hx, w_ref[:, wq + 2 * wkv:], preferred_element_type=F32).astype(BF16)


def _rope_tables(n_tok, n_ctx):
    t = jnp.arange(n_tok)
    row_ids = (t // GRID_W).astype(F32)
    col_ids = (t % GRID_W).astype(F32)
    half = ATTN_HEAD_DIM // 2
    inv = ROPE_THETA ** (-jnp.arange(0, half, 2, dtype=F32) / half)
    ang = jnp.concatenate([row_ids[:, None] * inv, col_ids[:, None] * inv], axis=-1)
    c, s = jnp.cos(ang), jnp.sin(ang)
    cos_h = jnp.concatenate([c, c], axis=-1)
    sin_h = jnp.concatenate([-s, s], axis=-1)
    reps = LANES // ATTN_HEAD_DIM
    cos2 = jnp.concatenate([jnp.ones((n_ctx, LANES), F32), jnp.tile(cos_h, (1, reps))], axis=0)
    sin2 = jnp.concatenate([jnp.zeros((n_ctx, LANES), F32), jnp.tile(sin_h, (1, reps))], axis=0)
    return cos2, sin2


def _attn_inproj(X, mod2, w_in, q_norm, k_norm, cos2, sin2, nct_rows):
    n_rows, d_model = X.shape
    tm = ROW_TILE
    ntiles, nct = n_rows // tm, nct_rows // tm
    hd = ATTN_HEAD_DIM
    wq, wkv = ATTN_HEADS * hd, ATTN_KV_HEADS * hd
    idx = jnp.arange(wq) // hd
    gmean = (idx[:, None] == idx[None, :]).astype(BF16) * (1.0 / hd)
    qn = jnp.tile(q_norm.astype(F32), ATTN_HEADS).reshape(1, wq)
    kn = jnp.tile(k_norm.astype(F32), ATTN_KV_HEADS).reshape(1, wkv)
    w = w_in.astype(BF16)
    kern = functools.partial(_attn_inproj_kernel, wq=wq, wkv=wkv)
    return pl.pallas_call(
        kern,
        grid=(ntiles,),
        in_specs=[
            pl.BlockSpec((tm, d_model), lambda i: (i, 0)),
            _mod_spec(d_model, nct),
            _resident(w.shape), _resident(gmean.shape), _resident(qn.shape), _resident(kn.shape),
            pl.BlockSpec((tm, LANES), lambda i: (i, 0)),
            pl.BlockSpec((tm, LANES), lambda i: (i, 0)),
        ],
        out_specs=[
            pl.BlockSpec((wq, tm), lambda i: (0, i)),
            pl.BlockSpec((ATTN_KV_HEADS, tm, hd), lambda i: (0, i, 0)),
            pl.BlockSpec((ATTN_KV_HEADS, V_ROWS, tm), lambda i: (0, 0, i)),
            pl.BlockSpec((tm, wq), lambda i: (i, 0)),
        ],
        out_shape=[
            jax.ShapeDtypeStruct((wq, n_rows), BF16),
            jax.ShapeDtypeStruct((ATTN_KV_HEADS, n_rows, hd), BF16),
            jax.ShapeDtypeStruct((ATTN_KV_HEADS, V_ROWS, n_rows), BF16),
            jax.ShapeDtypeStruct((n_rows, wq), BF16),
        ],
        name="attn_inproj",
    )(X, mod2, w, gmean, qn, kn, cos2, sin2)


def _flash_kernel(qT_ref, k_ref, vT_ref, o_ref, s_scr, m_scr, acc_scr, *, bk, nct_tiles, nct_chunks, n_chunks,
                  bounded):
    qi = pl.program_id(1)
    hd = ATTN_HEAD_DIM
    n_rep = ATTN_HEADS // ATTN_KV_HEADS
    m_scr[...] = jnp.full_like(m_scr, -jnp.inf)
    acc_scr[...] = jnp.zeros_like(acc_scr)
    nch = jnp.where(qi < nct_tiles, nct_chunks, n_chunks)

    def produce(buf, j):
        off = pl.multiple_of(jnp.minimum(j, nch - 1) * bk, bk)
        kb = k_ref[0, pl.ds(off, bk), :]
        for g in range(n_rep):
            s_scr[buf, g] = jnp.dot(kb, qT_ref[g * hd:(g + 1) * hd, :], preferred_element_type=F32)

    def consume(buf, j):
        off = pl.multiple_of(j * bk, bk)
        vb = vT_ref[0, :, pl.ds(off, bk)]
        for g in range(n_rep):
            s = s_scr[buf, g]
            if bounded:
                acc_scr[g] += jnp.dot(vb, jnp.exp2(s).astype(BF16), preferred_element_type=F32)
                continue
            m_old = m_scr[g]
            m_new = jnp.maximum(m_old, jnp.max(s, axis=0, keepdims=True))
            alpha = jnp.exp2(m_old - m_new)
            p = jnp.exp2(s - m_new).astype(BF16)
            acc_scr[g] = alpha * acc_scr[g] + jnp.dot(vb, p, preferred_element_type=F32)
            m_scr[g] = m_new

    last = nch - 1
    produce(0, last)
    produce(1, 0)
    consume(0, last)
    unroll = FLASH_UNROLL_BOUNDED if bounded else FLASH_UNROLL
    n_trips = last // unroll

    def trip(t, carry):
        for u in range(unroll):
            j = unroll * t + u
            produce(u % 2, j + 1)
            consume((u + 1) % 2, j)
        return carry

    lax.fori_loop(0, n_trips, trip, 0)

    def leftover(j, carry):
        produce(0, j)
        consume(0, j)
        return carry

    lax.fori_loop(n_trips * unroll, last, leftover, 0)

    o_t = jnp.concatenate([acc_scr[g, 0:hd] / acc_scr[g, hd:hd + 1] for g in range(n_rep)], axis=0)
    o_ref[...] = o_t.T.astype(BF16)


def _flash_attention(qT, k, vT, nct_rows, scores_bounded):
    wq, n_rows = qT.shape
    hd = ATTN_HEAD_DIM
    n_rep = ATTN_HEADS // ATTN_KV_HEADS
    bq = bk = ROW_TILE
    static = dict(bk=bk, nct_tiles=nct_rows // bq, nct_chunks=nct_rows // bk, n_chunks=n_rows // bk)

    def call(bounded, name):
        return pl.pallas_call(
            functools.partial(_flash_kernel, bounded=bounded, **static),
            grid=(ATTN_KV_HEADS, n_rows // bq),
            in_specs=[
                pl.BlockSpec((n_rep * hd, bq), lambda h, i: (h, i)),
                pl.BlockSpec((1, n_rows, hd), lambda h, i: (h, 0, 0)),
                pl.BlockSpec((1, V_ROWS, n_rows), lambda h, i: (h, 0, 0)),
            ],
            out_specs=pl.BlockSpec((bq, n_rep * hd), lambda h, i: (i, h)),
            out_shape=jax.ShapeDtypeStruct((n_rows, wq), BF16),
            scratch_shapes=[
                pltpu.VMEM((2, n_rep, bk, bq), F32),
                pltpu.VMEM((n_rep, 1, bq), F32),
                pltpu.VMEM((n_rep, V_ROWS, bq), F32),
            ],
            name=name,
        )

    return lax.cond(scores_bounded, call(True, "flash_attention_bounded"), call(False, "flash_attention"),
                    qT, k, vT)


def _attn_out_kernel(o_ref, g_ref, x_ref, mod_ref, wout_ref, lng_ref, lnb_ref, out_ref):
    t = o_ref[...].astype(F32) * _silu(g_ref[...].astype(F32))
    out = jnp.dot(t.astype(BF16), wout_ref[...], preferred_element_type=F32)
    gate = mod_ref[0][2:3]
    out_ref[...] = _layer_norm(DEEPNORM_ALPHA * x_ref[...] + gate * out, lng_ref[...], lnb_ref[...])


def _attn_out(o, g, X, mod2, w_out, ln_g, ln_b, nct_rows):
    n_rows, d_model = X.shape
    wq = o.shape[1]
    tm = ROW_TILE
    ntiles, nct = n_rows // tm, nct_rows // tm
    row = lambda w: pl.BlockSpec((tm, w), lambda i: (i, 0))
    return pl.pallas_call(
        _attn_out_kernel,
        grid=(ntiles,),
        in_specs=[row(wq), row(wq), row(d_model), _mod_spec(d_model, nct), _resident(w_out.shape),
                  _resident((1, d_model)), _resident((1, d_model))],
        out_specs=row(d_model),
        out_shape=jax.ShapeDtypeStruct((n_rows, d_model), F32),
        name="attn_out",
    )(o, g, X, mod2, w_out.astype(BF16), ln_g.reshape(1, d_model), ln_b.reshape(1, d_model))


def _attn_layer(X, mod2, w_in, q_norm, k_norm, w_out, ln_g, ln_b, cos2, sin2, nct_rows):
    qT, k, vT, g = _attn_inproj(X, mod2, w_in, q_norm, k_norm, cos2, sin2, nct_rows)
    score_bound = LOG2_E * ATTN_HEAD_DIM ** 0.5 * jnp.max(jnp.abs(q_norm)) * jnp.max(jnp.abs(k_norm))
    o = _flash_attention(qT, k, vT, nct_rows, score_bound <= SAFE_LOG2_SCORE)
    return _attn_out(o, g, X, mod2, w_out, ln_g, ln_b, nct_rows)


def _pool_kernel(xp_ref, xm_ref, xn_ref, mod_ref, win_ref, gw_ref, ls_ref, wout_ref, lng_ref, lnb_ref,
                 o_ref, ext_scr, t_scr, *, tm, nct, ntiles, n_ctx, n_tok):
    i = pl.program_id(0)
    m = mod_ref[0]
    shift, scale, gate = m[0:1], m[1:2], m[2:3]

    def modulate(v):
        return v * (1.0 + scale) + shift

    x_m = xm_ref[...]
    hx_m = modulate(x_m)
    hx_ext = jnp.concatenate([modulate(xp_ref[...]), hx_m, modulate(xn_ref[...])], axis=0).astype(BF16)
    hx_mb = hx_m.astype(BF16)
    maskf = _halo_row_mask(i, tm, nct, ntiles)

    is_ctx = i < nct
    t0 = (i - jnp.where(is_ctx, 0, nct)) * tm
    n_seq = jnp.where(is_ctx, n_ctx, n_tok)
    pos = t0 + lax.broadcasted_iota(jnp.int32, (tm, 1), 0)

    n_groups = len(POOL_WINDOWS)
    gd = gw_ref.shape[1]
    width = n_groups * gd
    for gi, window in enumerate(POOL_WINDOWS):
        cs = slice(gi * gd, (gi + 1) * gd)
        ext_scr[...] = jnp.dot(hx_ext, win_ref[:, cs], preferred_element_type=F32) * maskf
        lo_off = window // 2
        hi_off = window - 1 - lo_off
        win_sum = ext_scr[pl.ds(HALO - lo_off, tm), :]
        for o in range(-lo_off + 1, hi_off + 1):
            win_sum = win_sum + ext_scr[pl.ds(HALO + o, tm), :]
        cnt = (jnp.minimum(pos + hi_off + 1, n_seq) - jnp.maximum(pos - lo_off, 0)).astype(F32)
        pooled = win_sum / cnt - ext_scr[pl.ds(HALO, tm), :]
        mixed = jnp.dot(pooled.astype(BF16), gw_ref[gi], preferred_element_type=F32)
        zc = jnp.dot(hx_mb, win_ref[:, width + gi * gd:width + (gi + 1) * gd], preferred_element_type=F32)
        t_scr[:, cs] = ((mixed * ls_ref[:, cs]) * _silu(zc)).astype(BF16)
    out = jnp.dot(t_scr[...], wout_ref[...], preferred_element_type=F32)
    o_ref[...] = _layer_norm(DEEPNORM_ALPHA * x_m + gate * out, lng_ref[...], lnb_ref[...])


def _pool_layer(X, mod2, w_in, group_w, layer_scale, w_out, ln_g, ln_b, nct_rows):
    n_rows, d_model = X.shape
    tm = ROW_TILE
    ntiles, nct = n_rows // tm, nct_rows // tm
    width = w_out.shape[0]
    gd = group_w.shape[1]
    assert max(POOL_WINDOWS) // 2 <= HALO
    prev, nxt = _halo_specs(d_model, tm, n_rows)
    kern = functools.partial(_pool_kernel, tm=tm, nct=nct, ntiles=ntiles, n_ctx=nct_rows,
                             n_tok=n_rows - nct_rows)
    return pl.pallas_call(
        kern,
        grid=(ntiles,),
        in_specs=[
            prev,
            pl.BlockSpec((tm, d_model), lambda i: (i, 0)),
            nxt,
            _mod_spec(d_model, nct),
            _resident(w_in.shape), _resident(group_w.shape), _resident((1, width)), _resident(w_out.shape),
            _resident((1, d_model)), _resident((1, d_model)),
        ],
        out_specs=pl.BlockSpec((tm, d_model), lambda i: (i, 0)),
        out_shape=jax.ShapeDtypeStruct((n_rows, d_model), F32),
        scratch_shapes=[pltpu.VMEM((tm + 2 * HALO, gd), F32), pltpu.VMEM((tm, width), BF16)],
        name="pool_mixer",
    )(X, X, X, mod2, w_in.astype(BF16), group_w.astype(BF16), layer_scale.reshape(1, width),
      w_out.astype(BF16), ln_g.reshape(1, d_model), ln_b.reshape(1, d_model))


def kernel(x, c, ctx, c_ctx, mod_w, mod_b, ln_g, ln_b, ssm_w_in, ssm_conv_w, ssm_conv_b, ssm_dt_bias,
           ssm_a_log, ssm_d, ssm_norm_g, ssm_w_out, attn_w_in, attn_q_norm, attn_k_norm, attn_w_out,
           pool_w_in, pool_group_w, pool_scale, pool_w_out):
    bsz, n_tok, d_model = x.shape
    n_ctx = ctx.shape[1]
    assert bsz == 1 and n_ctx % ROW_TILE == 0 and n_tok % ROW_TILE == 0
    X = jnp.concatenate([ctx[0], x[0]], axis=0)
    mods = _mod_vectors(c, c_ctx, mod_w, mod_b)
    cos2, sin2 = _rope_tables(n_tok, n_ctx)
    for i in range(DEPTH):
        kind, j = i % N_MIXERS, i // N_MIXERS
        m3 = mods[i, 0:2].reshape(2, 3, d_model)
        mod2 = jnp.zeros((2, 8, d_model), F32).at[:, 0:3].set(m3)
        if kind == 0:
            X = _mamba_layer(X, mod2, ssm_w_in[j], ssm_conv_w[j], ssm_conv_b[j], ssm_dt_bias[j],
                             ssm_a_log[j], ssm_d[j], ssm_norm_g[j], ssm_w_out[j], ln_g[i], ln_b[i], n_ctx,
                             latent_only=(i == DEPTH - 1))
        elif kind == 1:
            X = _attn_layer(X, mod2, attn_w_in[j], attn_q_norm[j], attn_k_norm[j], attn_w_out[j],
                            ln_g[i], ln_b[i], cos2, sin2, n_ctx)
        else:
            X = _pool_layer(X, mod2, pool_w_in[j], pool_group_w[j], pool_scale[j], pool_w_out[j],
                            ln_g[i], ln_b[i], n_ctx)
    if (DEPTH - 1) % N_MIXERS != 0:
        X = X[n_ctx:]
    return X[None]
```

```python
import functools

import jax
import jax.numpy as jnp
from jax import lax
from jax.experimental import pallas as pl
from jax.experimental.pallas import tpu as pltpu

F32 = jnp.float32
BF16 = jnp.bfloat16
HIGHEST = lax.Precision.HIGHEST

DEPTH = 4
N_MIXERS = 3
GRID_W = 64
ROPE_THETA = 10000.0

SSM_HEAD_DIM = 64
SSM_STATE = 128
SSM_GROUPS = 8
SSM_CONV = 5
SSM_CHUNK = 256
SSD_TRI_BLOCKS = 2

ATTN_HEADS = 16
ATTN_KV_HEADS = 4
ATTN_HEAD_DIM = 64
V_ROWS = ATTN_HEAD_DIM + 16
LOG2_E = 1.4426950408889634
SAFE_LOG2_SCORE = 60.0
FLASH_UNROLL = 8
FLASH_UNROLL_BOUNDED = 32

POOL_WINDOWS = (2, 4, 8, 16)

DEEPNORM_ALPHA = (2 * DEPTH) ** 0.25
LN_EPS = 1e-5
RMS_EPS = 1e-6

ROW_TILE = 256
HALO = 8
LANES = 128
COL_CHUNK = 1024


def _silu(v):
    return v * jax.nn.sigmoid(v)


def _layer_norm(r, g, b):
    mu = jnp.mean(r, axis=-1, keepdims=True)
    d = r - mu
    var = jnp.mean(d * d, axis=-1, keepdims=True)
    return d * lax.rsqrt(var + LN_EPS) * g + b


def _resident(shape):
    nd = len(shape)
    return pl.BlockSpec(shape, lambda *_: (0,) * nd, pipeline_mode=pl.Buffered(1))


def _mod_spec(d_model, nct):
    return pl.BlockSpec((1, 8, d_model), lambda i: (jnp.where(i < nct, 1, 0), 0, 0))


def _halo_specs(d_model, tm, n_rows):
    per = tm // HALO
    last = n_rows // HALO - 1
    prev = pl.BlockSpec((HALO, d_model), lambda i: (jnp.maximum(i * per - 1, 0), 0))
    nxt = pl.BlockSpec((HALO, d_model), lambda i: (jnp.minimum((i + 1) * per, last), 0))
    return prev, nxt


def _halo_ok(i, nct, ntiles):
    prev_ok = jnp.logical_and(i != 0, i != nct)
    next_ok = jnp.logical_and(i != nct - 1, i != ntiles - 1)
    return prev_ok, next_ok


def _halo_row_mask(i, tm, nct, ntiles):
    prev_ok, next_ok = _halo_ok(i, nct, ntiles)
    lo = jnp.where(prev_ok, 0, HALO)
    hi = jnp.where(next_ok, tm + 2 * HALO, tm + HALO)
    rows = lax.broadcasted_iota(jnp.int32, (tm + 2 * HALO, 1), 0)
    return jnp.logical_and(rows >= lo, rows < hi).astype(F32)


def _mod_kernel(cT_ref, w_ref, b_ref, o_ref):
    s = _silu(cT_ref[...])
    w = w_ref[0]
    rows = [jnp.sum(w * s[:, r:r + 1], axis=0, keepdims=True) for r in range(2)]
    pad = jnp.zeros((o_ref.shape[1] - 2, w.shape[1]), F32)
    o_ref[0] = jnp.concatenate(rows + [pad], axis=0) + b_ref[0]


def _mod_vectors(c, c_ctx, mod_w, mod_b):
    depth, d_model, d3 = mod_w.shape
    cpad = jnp.zeros((d_model, 8), F32).at[:, 0].set(c[0]).at[:, 1].set(c_ctx)
    out = pl.pallas_call(
        _mod_kernel,
        grid=(depth, d3 // d_model),
        in_specs=[
            pl.BlockSpec((d_model, 8), lambda i, j: (0, 0)),
            pl.BlockSpec((1, d_model, d_model), lambda i, j: (i, 0, j)),
            pl.BlockSpec((1, 1, d_model), lambda i, j: (i, 0, j)),
        ],
        out_specs=pl.BlockSpec((1, 8, d_model), lambda i, j: (i, 0, j)),
        out_shape=jax.ShapeDtypeStruct((depth, 8, d3), F32),
        name="mod_vectors",
    )(cpad, mod_w, mod_b.reshape(depth, 1, d3))
    return out


def _mamba_inproj_kernel(xp_ref, xm_ref, xn_ref, mod_ref, wz_ref, wxbc_ref, wdt_ref, cw_ref, cb_ref,
                         dtb_ref, z_ref, xbc_ref, dt_ref, dtT_ref, xs_scr, out_scr, *, tm, nct, ntiles):
    i = pl.program_id(0)
    m = mod_ref[0]
    shift, scale = m[0:1], m[1:2]

    def modulate(v):
        return v * (1.0 + scale) + shift

    hx_m = modulate(xm_ref[...])
    hx_mb = hx_m.astype(BF16)
    pad = SSM_CONV // 2

    nph = HALO
    per = tm // nph
    blk = per + HALO
    prev_ok, next_ok = _halo_ok(i, nct, ntiles)
    lead_row = lax.broadcasted_iota(jnp.int32, (HALO, 1), 0)
    next_row = jnp.where(next_ok, 0, -1)
    prev_row = jnp.where(prev_ok, HALO - 1, -1)
    n_slab = xm_ref.shape[1] // LANES
    for l in range(n_slab):
        xs_scr[l] = hx_m[:, l * LANES:(l + 1) * LANES]
    pieces = []
    for b in range(nph):
        lead = jnp.where(lead_row == next_row, modulate(xn_ref[b:b + 1, :]),
                         jnp.where(lead_row == prev_row, modulate(xp_ref[b:b + 1, :]), 0.0))
        body = jnp.concatenate([xs_scr[l, pl.ds(b, per, stride=nph), :] for l in range(n_slab)], axis=1)
        pieces += [lead, body]
    hx_perm = jnp.concatenate(pieces, axis=0).astype(BF16)

    n_xbc = wxbc_ref.shape[1]
    for c in range(n_xbc // COL_CHUNK):
        cs = slice(c * COL_CHUNK, (c + 1) * COL_CHUNK)
        pre = jnp.dot(hx_perm, wxbc_ref[:, cs], preferred_element_type=F32)
        blocks = {b: pre[b * blk:(b + 1) * blk] for b in range(nph)}
        for b in range(pad):
            blocks[b + nph] = pltpu.roll(blocks[b], blk - 1, 0)
            blocks[-1 - b] = pltpu.roll(blocks[nph - 1 - b], 1, 0)
        for b in range(nph):
            acc = cb_ref[:, cs] + cw_ref[0:1, cs] * blocks[b - pad][HALO:]
            for k in range(1, SSM_CONV):
                acc = acc + cw_ref[k:k + 1, cs] * blocks[b + k - pad][HALO:]
            act = _silu(acc)
            for l in range(COL_CHUNK // LANES):
                out_scr[c % 2, l, pl.ds(b, per, stride=nph), :] = act[:, l * LANES:(l + 1) * LANES]
        xbc_ref[:, cs] = jnp.concatenate([out_scr[c % 2, l] for l in range(COL_CHUNK // LANES)],
                                         axis=1).astype(BF16)

    n_z = wz_ref.shape[1]
    for c in range(n_z // COL_CHUNK):
        cs = slice(c * COL_CHUNK, (c + 1) * COL_CHUNK)
        z_ref[:, cs] = jnp.dot(hx_mb, wz_ref[:, cs], preferred_element_type=F32).astype(BF16)

    dt_raw = jnp.dot(hx_m, wdt_ref[...], precision=HIGHEST, preferred_element_type=F32) + dtb_ref[...]
    dt = jnp.maximum(dt_raw, 0.0) + jnp.log1p(jnp.exp(-jnp.abs(dt_raw)))
    dt_ref[...] = dt
    dtT_ref[...] = dt.T


def _mamba_inproj(X, mod2, w_in, conv_w, conv_b, dt_bias, nct_rows):
    n_rows, d_model = X.shape
    tm = ROW_TILE
    ntiles, nct = n_rows // tm, nct_rows // tm
    d_inner = w_in.shape[1] - conv_w.shape[1] - 2 * (dt_bias.shape[-1])
    n_xbc = conv_w.shape[1]
    n_dt = 2 * dt_bias.shape[-1]
    wz = w_in[:, :d_inner].astype(BF16)
    wxbc = w_in[:, d_inner:d_inner + n_xbc].astype(BF16)
    wdt = jnp.zeros((d_model, LANES), F32).at[:, :n_dt].set(w_in[:, d_inner + n_xbc:])
    dtb = jnp.zeros((1, LANES), F32).at[0, :n_dt].set(dt_bias.reshape(-1))
    cw = jnp.zeros((8, n_xbc), F32).at[:SSM_CONV].set(conv_w)
    prev, nxt = _halo_specs(d_model, tm, n_rows)
    kern = functools.partial(_mamba_inproj_kernel, tm=tm, nct=nct, ntiles=ntiles)
    return pl.pallas_call(
        kern,
        grid=(ntiles,),
        in_specs=[
            prev,
            pl.BlockSpec((tm, d_model), lambda i: (i, 0)),
            nxt,
            _mod_spec(d_model, nct),
            _resident(wz.shape), _resident(wxbc.shape), _resident(wdt.shape),
            _resident(cw.shape), _resident((1, n_xbc)), _resident(dtb.shape),
        ],
        out_specs=[
            pl.BlockSpec((tm, d_inner), lambda i: (i, 0)),
            pl.BlockSpec((tm, n_xbc), lambda i: (i, 0)),
            pl.BlockSpec((tm, LANES), lambda i: (i, 0)),
            pl.BlockSpec((LANES, tm), lambda i: (0, i)),
        ],
        out_shape=[
            jax.ShapeDtypeStruct((n_rows, d_inner), BF16),
            jax.ShapeDtypeStruct((n_rows, n_xbc), BF16),
            jax.ShapeDtypeStruct((n_rows, LANES), F32),
            jax.ShapeDtypeStruct((LANES, n_rows), F32),
        ],
        scratch_shapes=[pltpu.VMEM((d_model // LANES, tm, LANES), F32),
                        pltpu.VMEM((2, COL_CHUNK // LANES, tm, LANES), F32)],
        name="mamba_inproj",
    )(X, X, X, mod2, wz, wxbc, wdt, cw, conv_b.reshape(1, n_xbc), dtb)


def _split3(v):
    hi = v.astype(BF16)
    r1 = v - hi.astype(F32)
    mid = r1.astype(BF16)
    lo = (r1 - mid.astype(F32)).astype(BF16)
    return hi, mid, lo


def _ssd_direction(d, xs_ref, b_ref, c_ref, dt_ref, dtT_ref, arow_ref, acol_ref, r2_ref, y_ref,
                   state_scr, lower, upper, lane_head, *, L, n_heads):
    n_groups = SSM_GROUPS
    hpg = n_heads // n_groups
    gw = hpg * SSM_HEAD_DIM
    ns = SSM_STATE
    dt = dt_ref[...]
    dA = dt * arow_ref[0:1, :]
    dAT = dtT_ref[...] * acol_ref[...]
    low3, up3 = (jnp.concatenate([t.astype(BF16)] * 3, axis=1) for t in (lower, upper))
    low3v, up3v = (jnp.concatenate([t.astype(BF16)] * 3, axis=0) for t in (lower, upper))
    dA3 = jnp.concatenate(_split3(dA), axis=0)
    dAT3 = jnp.concatenate(_split3(dAT), axis=1)
    if d == 0:
        cum = jnp.dot(low3, dA3, preferred_element_type=F32)
        cumT = jnp.dot(dAT3, up3v, preferred_element_type=F32)
        tot = cum[L - 1:L]
        mask = lower
    else:
        cum = jnp.dot(up3, dA3, preferred_element_type=F32)
        cumT = jnp.dot(dAT3, low3v, preferred_element_type=F32)
        tot = cum[0:1]
        mask = upper
    e_in = jnp.exp2(cum)
    tail = jnp.exp2(tot - cum)
    st = jnp.concatenate([dt * tail, e_in], axis=0)
    hi = st.astype(BF16)
    lo = (st - hi.astype(F32)).astype(BF16)
    lhs = jnp.concatenate([hi, lo], axis=1)
    src_row = cumT - jnp.log2(dtT_ref[...])
    neg_inf = jnp.float32(-jnp.inf)

    for g in range(n_groups):
        gs = slice(g * gw, (g + 1) * gw)
        rep = jnp.dot(lhs, r2_ref[:, gs], preferred_element_type=F32)
        xs_b = xs_ref[:, gs]
        xdt_tail = (xs_b.astype(F32) * rep[0:L]).astype(BF16)
        e_rep = rep[L:2 * L]
        bg = b_ref[:, g * ns:(g + 1) * ns]
        cg = c_ref[:, g * ns:(g + 1) * ns]
        cb = lax.dot_general(cg, bg, (((1,), (1,)), ((), ())), preferred_element_type=F32)
        bs = L // SSD_TRI_BLOCKS
        xks = [jnp.where(lane_head == k, xs_b, jnp.zeros_like(xs_b)) for k in range(hpg)]
        y_blocks = []
        for i in range(SSD_TRI_BLOCKS):
            rows = slice(i * bs, (i + 1) * bs)
            cols = slice(0, (i + 1) * bs) if d == 0 else slice(i * bs, L)
            scs = []
            for k in range(hpg):
                lane = d * n_heads + g * hpg + k
                seg = cum[rows, lane:lane + 1] - src_row[lane:lane + 1, cols]
                dec_dt = jnp.exp2(jnp.where(mask[rows, cols], seg, neg_inf))
                scs.append((cb[rows, cols] * dec_dt).astype(BF16))
            y_blocks.append(jnp.dot(jnp.concatenate(scs, axis=1), jnp.concatenate([x[cols] for x in xks], axis=0),
                                    preferred_element_type=F32))
        y = jnp.concatenate(y_blocks, axis=0)
        h_t = state_scr[d, g]
        y_off = jnp.dot(cg, h_t.astype(BF16), preferred_element_type=F32) * e_rep
        e_tot = e_rep[L - 1:L] if d == 0 else e_rep[0:1]
        upd = lax.dot_general(bg, xdt_tail, (((0,), (0,)), ((), ())), preferred_element_type=F32)
        state_scr[d, g] = h_t * e_tot + upd
        y_ref[:, gs] = (y + y_off).astype(BF16)


def _ssd_kernel(xsf, bf, cf, dtf, dtTf, xsb, bb, cb_, dtb, dtTb, arow_ref, acol_ref, r2f_ref, r2b_ref,
                yf_ref, yb_ref, state_scr, *, L, n_heads):
    @pl.when(pl.program_id(0) == 0)
    def _():
        state_scr[...] = jnp.zeros_like(state_scr)

    r = lax.broadcasted_iota(jnp.int32, (L, L), 0)
    c = lax.broadcasted_iota(jnp.int32, (L, L), 1)
    lower, upper = c <= r, r <= c
    gw = (n_heads // SSM_GROUPS) * SSM_HEAD_DIM
    lane_head = lax.broadcasted_iota(jnp.int32, (1, gw), 1) // SSM_HEAD_DIM
    common = dict(L=L, n_heads=n_heads)
    _ssd_direction(0, xsf, bf, cf, dtf, dtTf, arow_ref, acol_ref, r2f_ref, yf_ref, state_scr,
                   lower, upper, lane_head, **common)
    _ssd_direction(1, xsb, bb, cb_, dtb, dtTb, arow_ref, acol_ref, r2b_ref, yb_ref, state_scr,
                   lower, upper, lane_head, **common)


def _ssd_scan(xbc, dt, dtT, a_log, d_inner):
    n_rows = xbc.shape[0]
    L = SSM_CHUNK
    n = n_rows // L
    n_heads = a_log.shape[-1]
    n_bc = SSM_GROUPS * SSM_STATE
    a = -jnp.exp(a_log.astype(F32)).reshape(-1) * LOG2_E
    a_pad = jnp.zeros((LANES,), F32).at[:2 * n_heads].set(a)
    arow = jnp.zeros((8, LANES), F32).at[0].set(a_pad)
    acol = jnp.broadcast_to(a_pad[:, None], (LANES, L))
    head_of_col = jnp.arange(d_inner) // SSM_HEAD_DIM
    r2 = []
    for d in range(2):
        sel = (jnp.arange(LANES)[:, None] == (d * n_heads + head_of_col)[None, :]).astype(BF16)
        r2.append(jnp.concatenate([sel, sel], axis=0))
    fwd = lambda i: i
    bwd = lambda i: jnp.where(i == 0, 0, n - i)
    assert n_rows // L >= 2 and L == ROW_TILE

    def specs(cm):
        return [
            pl.BlockSpec((L, d_inner), lambda i: (cm(i), 0)),
            pl.BlockSpec((L, n_bc), lambda i: (cm(i), d_inner // n_bc)),
            pl.BlockSpec((L, n_bc), lambda i: (cm(i), d_inner // n_bc + 1)),
            pl.BlockSpec((L, LANES), lambda i: (cm(i), 0)),
            pl.BlockSpec((LANES, L), lambda i: (0, cm(i))),
        ]

    kern = functools.partial(_ssd_kernel, L=L, n_heads=n_heads)
    return pl.pallas_call(
        kern,
        grid=(n,),
        in_specs=specs(fwd) + specs(bwd) + [
            _resident(arow.shape), _resident(acol.shape), _resident(r2[0].shape), _resident(r2[1].shape)],
        out_specs=[
            pl.BlockSpec((L, d_inner), lambda i: (fwd(i), 0)),
            pl.BlockSpec((L, d_inner), lambda i: (bwd(i), 0)),
        ],
        out_shape=[jax.ShapeDtypeStruct((n_rows, d_inner), BF16)] * 2,
        scratch_shapes=[pltpu.VMEM((2, SSM_GROUPS, SSM_STATE, d_inner // SSM_GROUPS), F32)],
        compiler_params=pltpu.CompilerParams(dimension_semantics=("arbitrary",)),
        name="ssd_scan",
    )(xbc, xbc, xbc, dt, dtT, xbc, xbc, xbc, dt, dtT, arow, acol, r2[0], r2[1])


def _mamba_finish_kernel(yf_ref, yb_ref, xs_ref, z_ref, x_ref, mod_ref, drep_ref, ng_ref, wout_ref,
                         lng_ref, lnb_ref, o_ref):
    y = yf_ref[...].astype(F32) + yb_ref[...].astype(F32) + xs_ref[...].astype(F32) * drep_ref[...]
    t = y * _silu(z_ref[...].astype(F32))
    t = t * lax.rsqrt(jnp.mean(t * t, axis=-1, keepdims=True) + RMS_EPS) * ng_ref[...]
    out = jnp.dot(t.astype(BF16), wout_ref[...], preferred_element_type=F32)
    gate = mod_ref[0][2:3]
    o_ref[...] = _layer_norm(DEEPNORM_ALPHA * x_ref[...] + gate * out, lng_ref[...], lnb_ref[...])


def _mamba_finish(yf, yb, xbc, z, X, mod2, d_skip, norm_g, w_out, ln_g, ln_b, nct_rows, latent_only):
    n_rows, d_model = X.shape
    d_inner = z.shape[1]
    tm = ROW_TILE
    ntiles, nct = n_rows // tm, nct_rows // tm
    skip = nct if latent_only else 0
    drep = jnp.repeat(d_skip.astype(F32), SSM_HEAD_DIM).reshape(1, d_inner)
    row = lambda w: pl.BlockSpec((tm, w), lambda i: (i + skip, 0))
    return pl.pallas_call(
        _mamba_finish_kernel,
        grid=(ntiles - skip,),
        in_specs=[row(d_inner), row(d_inner), row(d_inner), row(d_inner), row(d_model),
                  _mod_spec(d_model, nct - skip), _resident((1, d_inner)), _resident((1, d_inner)),
                  _resident(w_out.shape), _resident((1, d_model)), _resident((1, d_model))],
        out_specs=pl.BlockSpec((tm, d_model), lambda i: (i, 0)),
        out_shape=jax.ShapeDtypeStruct((n_rows - skip * tm, d_model), F32),
        name="mamba_finish",
    )(yf, yb, xbc, z, X, mod2, drep, norm_g.reshape(1, d_inner), w_out.astype(BF16),
      ln_g.reshape(1, d_model), ln_b.reshape(1, d_model))


def _mamba_layer(X, mod2, w_in, conv_w, conv_b, dt_bias, a_log, d_skip, norm_g, w_out, ln_g, ln_b, nct_rows,
                 latent_only):
    z, xbc, dt, dtT = _mamba_inproj(X, mod2, w_in, conv_w, conv_b, dt_bias, nct_rows)
    yf, yb = _ssd_scan(xbc, dt, dtT, a_log, z.shape[1])
    return _mamba_finish(yf, yb, xbc, z, X, mod2, d_skip, norm_g, w_out, ln_g, ln_b, nct_rows, latent_only)


def _attn_inproj_kernel(x_ref, mod_ref, w_ref, gm_ref, qn_ref, kn_ref, cos_ref, sin_ref,
                        qT_ref, k_ref, vT_ref, g_ref, *, wq, wkv):
    m = mod_ref[0]
    hx = (x_ref[...] * (1.0 + m[1:2]) + m[0:1]).astype(BF16)
    hd = ATTN_HEAD_DIM

    def norm_rope(t, gmean, gain):
        width = t.shape[1]
        ms = jnp.dot((t * t).astype(BF16), gmean, preferred_element_type=F32)
        tn = t * lax.rsqrt(ms + RMS_EPS) * gain
        reps = width // LANES
        cs = jnp.concatenate([cos_ref[...]] * reps, axis=1)
        sn = jnp.concatenate([sin_ref[...]] * reps, axis=1)
        lane = lax.broadcasted_iota(jnp.int32, (1, width), 1)
        first_half = (lane % hd) < hd // 2
        swapped = jnp.where(first_half, pltpu.roll(tn, width - hd // 2, 1), pltpu.roll(tn, hd // 2, 1))
        return tn * cs + swapped * sn

    q = jnp.dot(hx, w_ref[:, 0:wq], preferred_element_type=F32)
    qr = norm_rope(q, gm_ref[...], qn_ref[...]) * (hd ** -0.5 * LOG2_E)
    qT_ref[...] = qr.T.astype(BF16)
    k = jnp.dot(hx, w_ref[:, wq:wq + wkv], preferred_element_type=F32)
    kr = norm_rope(k, gm_ref[0:wkv, 0:wkv], kn_ref[...])
    for j in range(wkv // hd):
        k_ref[j] = kr[:, j * hd:(j + 1) * hd].astype(BF16)
    v = jnp.dot(hx, w_ref[:, wq + wkv:wq + 2 * wkv], preferred_element_type=F32)
    v_t = v.T.astype(BF16)
    for j in range(wkv // hd):
        vT_ref[j, 0:hd, :] = v_t[j * hd:(j + 1) * hd, :]
        vT_ref[j, hd:, :] = jnp.ones((V_ROWS - hd, v_t.shape[1]), BF16)
    g_ref[...] = jnp.dot(hx, w_ref[:, wq + 2 * wkv:], preferred_element_type=F32).astype(BF16)


def _rope_tables(n_tok, n_ctx):
    t = jnp.arange(n_tok)
    row_ids = (t // GRID_W).astype(F32)
    col_ids = (t % GRID_W).astype(F32)
    half = ATTN_HEAD_DIM // 2
    inv = ROPE_THETA ** (-jnp.arange(0, half, 2, dtype=F32) / half)
    ang = jnp.concatenate([row_ids[:, None] * inv, col_ids[:, None] * inv], axis=-1)
    c, s = jnp.cos(ang), jnp.sin(ang)
    cos_h = jnp.concatenate([c, c], axis=-1)
    sin_h = jnp.concatenate([-s, s], axis=-1)
    reps = LANES // ATTN_HEAD_DIM
    cos2 = jnp.concatenate([jnp.ones((n_ctx, LANES), F32), jnp.tile(cos_h, (1, reps))], axis=0)
    sin2 = jnp.concatenate([jnp.zeros((n_ctx, LANES), F32), jnp.tile(sin_h, (1, reps))], axis=0)
    return cos2, sin2


def _attn_inproj(X, mod2, w_in, q_norm, k_norm, cos2, sin2, nct_rows):
    n_rows, d_model = X.shape
    tm = ROW_TILE
    ntiles, nct = n_rows // tm, nct_rows // tm
    hd = ATTN_HEAD_DIM
    wq, wkv = ATTN_HEADS * hd, ATTN_KV_HEADS * hd
    idx = jnp.arange(wq) // hd
    gmean = (idx[:, None] == idx[None, :]).astype(BF16) * (1.0 / hd)
    qn = jnp.tile(q_norm.astype(F32), ATTN_HEADS).reshape(1, wq)
    kn = jnp.tile(k_norm.astype(F32), ATTN_KV_HEADS).reshape(1, wkv)
    w = w_in.astype(BF16)
    kern = functools.partial(_attn_inproj_kernel, wq=wq, wkv=wkv)
    return pl.pallas_call(
        kern,
        grid=(ntiles,),
        in_specs=[
            pl.BlockSpec((tm, d_model), lambda i: (i, 0)),
            _mod_spec(d_model, nct),
            _resident(w.shape), _resident(gmean.shape), _resident(qn.shape), _resident(kn.shape),
            pl.BlockSpec((tm, LANES), lambda i: (i, 0)),
            pl.BlockSpec((tm, LANES), lambda i: (i, 0)),
        ],
        out_specs=[
            pl.BlockSpec((wq, tm), lambda i: (0, i)),
            pl.BlockSpec((ATTN_KV_HEADS, tm, hd), lambda i: (0, i, 0)),
            pl.BlockSpec((ATTN_KV_HEADS, V_ROWS, tm), lambda i: (0, 0, i)),
            pl.BlockSpec((tm, wq), lambda i: (i, 0)),
        ],
        out_shape=[
            jax.ShapeDtypeStruct((wq, n_rows), BF16),
            jax.ShapeDtypeStruct((ATTN_KV_HEADS, n_rows, hd), BF16),
            jax.ShapeDtypeStruct((ATTN_KV_HEADS, V_ROWS, n_rows), BF16),
            jax.ShapeDtypeStruct((n_rows, wq), BF16),
        ],
        name="attn_inproj",
    )(X, mod2, w, gmean, qn, kn, cos2, sin2)


def _flash_kernel(qT_ref, k_ref, vT_ref, o_ref, s_scr, m_scr, acc_scr, *, bk, nct_tiles, nct_chunks, n_chunks,
                  bounded):
    qi = pl.program_id(1)
    hd = ATTN_HEAD_DIM
    n_rep = ATTN_HEADS // ATTN_KV_HEADS
    m_scr[...] = jnp.full_like(m_scr, -jnp.inf)
    acc_scr[...] = jnp.zeros_like(acc_scr)
    nch = jnp.where(qi < nct_tiles, nct_chunks, n_chunks)

    def produce(buf, j):
        off = pl.multiple_of(jnp.minimum(j, nch - 1) * bk, bk)
        kb = k_ref[0, pl.ds(off, bk), :]
        for g in range(n_rep):
            s_scr[buf, g] = jnp.dot(kb, qT_ref[g * hd:(g + 1) * hd, :], preferred_element_type=F32)

    def consume(buf, j):
        off = pl.multiple_of(j * bk, bk)
        vb = vT_ref[0, :, pl.ds(off, bk)]
        for g in range(n_rep):
            s = s_scr[buf, g]
            if bounded:
                acc_scr[g] += jnp.dot(vb, jnp.exp2(s).astype(BF16), preferred_element_type=F32)
                continue
            m_old = m_scr[g]
            m_new = jnp.maximum(m_old, jnp.max(s, axis=0, keepdims=True))
            alpha = jnp.exp2(m_old - m_new)
            p = jnp.exp2(s - m_new).astype(BF16)
            acc_scr[g] = alpha * acc_scr[g] + jnp.dot(vb, p, preferred_element_type=F32)
            m_scr[g] = m_new

    last = nch - 1
    produce(0, last)
    produce(1, 0)
    consume(0, last)
    unroll = FLASH_UNROLL_BOUNDED if bounded else FLASH_UNROLL
    n_trips = last // unroll

    def trip(t, carry):
        for u in range(unroll):
            j = unroll * t + u
            produce(u % 2, j + 1)
            consume((u + 1) % 2, j)
        return carry

    lax.fori_loop(0, n_trips, trip, 0)

    def leftover(j, carry):
        produce(0, j)
        consume(0, j)
        return carry

    lax.fori_loop(n_trips * unroll, last, leftover, 0)

    o_t = jnp.concatenate([acc_scr[g, 0:hd] / acc_scr[g, hd:hd + 1] for g in range(n_rep)], axis=0)
    o_ref[...] = o_t.T.astype(BF16)


def _flash_attention(qT, k, vT, nct_rows, scores_bounded):
    wq, n_rows = qT.shape
    hd = ATTN_HEAD_DIM
    n_rep = ATTN_HEADS // ATTN_KV_HEADS
    bq = bk = ROW_TILE
    static = dict(bk=bk, nct_tiles=nct_rows // bq, nct_chunks=nct_rows // bk, n_chunks=n_rows // bk)

    def call(bounded, name):
        return pl.pallas_call(
            functools.partial(_flash_kernel, bounded=bounded, **static),
            grid=(ATTN_KV_HEADS, n_rows // bq),
            in_specs=[
                pl.BlockSpec((n_rep * hd, bq), lambda h, i: (h, i)),
                pl.BlockSpec((1, n_rows, hd), lambda h, i: (h, 0, 0)),
                pl.BlockSpec((1, V_ROWS, n_rows), lambda h, i: (h, 0, 0)),
            ],
            out_specs=pl.BlockSpec((bq, n_rep * hd), lambda h, i: (i, h)),
            out_shape=jax.ShapeDtypeStruct((n_rows, wq), BF16),
            scratch_shapes=[
                pltpu.VMEM((2, n_rep, bk, bq), F32),
                pltpu.VMEM((n_rep, 1, bq), F32),
                pltpu.VMEM((n_rep, V_ROWS, bq), F32),
            ],
            name=name,
        )

    return lax.cond(scores_bounded, call(True, "flash_attention_bounded"), call(False, "flash_attention"),
                    qT, k, vT)


def _attn_out_kernel(o_ref, g_ref, x_ref, mod_ref, wout_ref, lng_ref, lnb_ref, out_ref):
    t = o_ref[...].astype(F32) * _silu(g_ref[...].astype(F32))
    out = jnp.dot(t.astype(BF16), wout_ref[...], preferred_element_type=F32)
    gate = mod_ref[0][2:3]
    out_ref[...] = _layer_norm(DEEPNORM_ALPHA * x_ref[...] + gate * out, lng_ref[...], lnb_ref[...])


def _attn_out(o, g, X, mod2, w_out, ln_g, ln_b, nct_rows):
    n_rows, d_model = X.shape
    wq = o.shape[1]
    tm = ROW_TILE
    ntiles, nct = n_rows // tm, nct_rows // tm
    row = lambda w: pl.BlockSpec((tm, w), lambda i: (i, 0))
    return pl.pallas_call(
        _attn_out_kernel,
        grid=(ntiles,),
        in_specs=[row(wq), row(wq), row(d_model), _mod_spec(d_model, nct), _resident(w_out.shape),
                  _resident((1, d_model)), _resident((1, d_model))],
        out_specs=row(d_model),
        out_shape=jax.ShapeDtypeStruct((n_rows, d_model), F32),
        name="attn_out",
    )(o, g, X, mod2, w_out.astype(BF16), ln_g.reshape(1, d_model), ln_b.reshape(1, d_model))


def _attn_layer(X, mod2, w_in, q_norm, k_norm, w_out, ln_g, ln_b, cos2, sin2, nct_rows):
    qT, k, vT, g = _attn_inproj(X, mod2, w_in, q_norm, k_norm, cos2, sin2, nct_rows)
    score_bound = LOG2_E * ATTN_HEAD_DIM ** 0.5 * jnp.max(jnp.abs(q_norm)) * jnp.max(jnp.abs(k_norm))
    o = _flash_attention(qT, k, vT, nct_rows, score_bound <= SAFE_LOG2_SCORE)
    return _attn_out(o, g, X, mod2, w_out, ln_g, ln_b, nct_rows)


def _pool_kernel(xp_ref, xm_ref, xn_ref, mod_ref, win_ref, gw_ref, ls_ref, wout_ref, lng_ref, lnb_ref,
                 o_ref, ext_scr, t_scr, *, tm, nct, ntiles, n_ctx, n_tok):
    i = pl.program_id(0)
    m = mod_ref[0]
    shift, scale, gate = m[0:1], m[1:2], m[2:3]

    def modulate(v):
        return v * (1.0 + scale) + shift

    x_m = xm_ref[...]
    hx_m = modulate(x_m)
    hx_ext = jnp.concatenate([modulate(xp_ref[...]), hx_m, modulate(xn_ref[...])], axis=0).astype(BF16)
    hx_mb = hx_m.astype(BF16)
    maskf = _halo_row_mask(i, tm, nct, ntiles)

    is_ctx = i < nct
    t0 = (i - jnp.where(is_ctx, 0, nct)) * tm
    n_seq = jnp.where(is_ctx, n_ctx, n_tok)
    pos = t0 + lax.broadcasted_iota(jnp.int32, (tm, 1), 0)

    n_groups = len(POOL_WINDOWS)
    gd = gw_ref.shape[1]
    width = n_groups * gd
    for gi, window in enumerate(POOL_WINDOWS):
        cs = slice(gi * gd, (gi + 1) * gd)
        ext_scr[...] = jnp.dot(hx_ext, win_ref[:, cs], preferred_element_type=F32) * maskf
        lo_off = window // 2
        hi_off = window - 1 - lo_off
        win_sum = ext_scr[pl.ds(HALO - lo_off, tm), :]
        for o in range(-lo_off + 1, hi_off + 1):
            win_sum = win_sum + ext_scr[pl.ds(HALO + o, tm), :]
        cnt = (jnp.minimum(pos + hi_off + 1, n_seq) - jnp.maximum(pos - lo_off, 0)).astype(F32)
        pooled = win_sum / cnt - ext_scr[pl.ds(HALO, tm), :]
        mixed = jnp.dot(pooled.astype(BF16), gw_ref[gi], preferred_element_type=F32)
        zc = jnp.dot(hx_mb, win_ref[:, width + gi * gd:width + (gi + 1) * gd], preferred_element_type=F32)
        t_scr[:, cs] = ((mixed * ls_ref[:, cs]) * _silu(zc)).astype(BF16)
    out = jnp.dot(t_scr[...], wout_ref[...], preferred_element_type=F32)
    o_ref[...] = _layer_norm(DEEPNORM_ALPHA * x_m + gate * out, lng_ref[...], lnb_ref[...])


def _pool_layer(X, mod2, w_in, group_w, layer_scale, w_out, ln_g, ln_b, nct_rows):
    n_rows, d_model = X.shape
    tm = ROW_TILE
    ntiles, nct = n_rows // tm, nct_rows // tm
    width = w_out.shape[0]
    gd = group_w.shape[1]
    assert max(POOL_WINDOWS) // 2 <= HALO
    prev, nxt = _halo_specs(d_model, tm, n_rows)
    kern = functools.partial(_pool_kernel, tm=tm, nct=nct, ntiles=ntiles, n_ctx=nct_rows,
                             n_tok=n_rows - nct_rows)
    return pl.pallas_call(
        kern,
        grid=(ntiles,),
        in_specs=[
            prev,
            pl.BlockSpec((tm, d_model), lambda i: (i, 0)),
            nxt,
            _mod_spec(d_model, nct),
            _resident(w_in.shape), _resident(group_w.shape), _resident((1, width)), _resident(w_out.shape),
            _resident((1, d_model)), _resident((1, d_model)),
        ],
        out_specs=pl.BlockSpec((tm, d_model), lambda i: (i, 0)),
        out_shape=jax.ShapeDtypeStruct((n_rows, d_model), F32),
        scratch_shapes=[pltpu.VMEM((tm + 2 * HALO, gd), F32), pltpu.VMEM((tm, width), BF16)],
        name="pool_mixer",
    )(X, X, X, mod2, w_in.astype(BF16), group_w.astype(BF16), layer_scale.reshape(1, width),
      w_out.astype(BF16), ln_g.reshape(1, d_model), ln_b.reshape(1, d_model))


def kernel(x, c, ctx, c_ctx, mod_w, mod_b, ln_g, ln_b, ssm_w_in, ssm_conv_w, ssm_conv_b, ssm_dt_bias,
           ssm_a_log, ssm_d, ssm_norm_g, ssm_w_out, attn_w_in, attn_q_norm, attn_k_norm, attn_w_out,
           pool_w_in, pool_group_w, pool_scale, pool_w_out):
    bsz, n_tok, d_model = x.shape
    n_ctx = ctx.shape[1]
    assert bsz == 1 and n_ctx % ROW_TILE == 0 and n_tok % ROW_TILE == 0
    X = jnp.concatenate([ctx[0], x[0]], axis=0)
    mods = _mod_vectors(c, c_ctx, mod_w, mod_b)
    cos2, sin2 = _rope_tables(n_tok, n_ctx)
    for i in range(DEPTH):
        kind, j = i % N_MIXERS, i // N_MIXERS
        m3 = mods[i, 0:2].reshape(2, 3, d_model)
        mod2 = jnp.zeros((2, 8, d_model), F32).at[:, 0:3].set(m3)
        if kind == 0:
            X = _mamba_layer(X, mod2, ssm_w_in[j], ssm_conv_w[j], ssm_conv_b[j], ssm_dt_bias[j],
                             ssm_a_log[j], ssm_d[j], ssm_norm_g[j], ssm_w_out[j], ln_g[i], ln_b[i], n_ctx,
                             latent_only=(i == DEPTH - 1))
        elif kind == 1:
            X = _attn_layer(X, mod2, attn_w_in[j], attn_q_norm[j], attn_k_norm[j], attn_w_out[j],
                            ln_g[i], ln_b[i], cos2, sin2, n_ctx)
        else:
            X = _pool_layer(X, mod2, pool_w_in[j], pool_group_w[j], pool_scale[j], pool_w_out[j],
                            ln_g[i], ln_b[i], n_ctx)
    if (DEPTH - 1) % N_MIXERS != 0:
        X = X[n_ctx:]
    return X[None]
```

```python
import functools

import jax
import jax.numpy as jnp
from jax import lax
from jax.experimental import pallas as pl
from jax.experimental.pallas import tpu as pltpu

F32 = jnp.float32
BF16 = jnp.bfloat16
HIGHEST = lax.Precision.HIGHEST

DEPTH = 4
N_MIXERS = 3
GRID_W = 64
ROPE_THETA = 10000.0

SSM_HEAD_DIM = 64
SSM_STATE = 128
SSM_GROUPS = 8
SSM_CONV = 5
SSM_CHUNK = 256
SSD_TRI_BLOCKS = 2

ATTN_HEADS = 16
ATTN_KV_HEADS = 4
ATTN_HEAD_DIM = 64
V_ROWS = ATTN_HEAD_DIM + 16
LOG2_E = 1.4426950408889634
SAFE_LOG2_SCORE = 60.0
FLASH_UNROLL = 8
FLASH_UNROLL_BOUNDED = 32

POOL_WINDOWS = (2, 4, 8, 16)

DEEPNORM_ALPHA = (2 * DEPTH) ** 0.25
LN_EPS = 1e-5
RMS_EPS = 1e-6

ROW_TILE = 256
HALO = 8
LANES = 128
COL_CHUNK = 1024
MOD_COL_BLOCKS = 2


def _silu(v):
    return v * jax.nn.sigmoid(v)


def _layer_norm(r, g, b):
    mu = jnp.mean(r, axis=-1, keepdims=True)
    d = r - mu
    var = jnp.mean(d * d, axis=-1, keepdims=True)
    return d * lax.rsqrt(var + LN_EPS) * g + b


def _resident(shape):
    nd = len(shape)
    return pl.BlockSpec(shape, lambda *_: (0,) * nd, pipeline_mode=pl.Buffered(1))


def _mod_spec(d_model, nct):
    return pl.BlockSpec((1, 8, d_model), lambda i: (jnp.where(i < nct, 1, 0), 0, 0))


def _halo_specs(d_model, tm, n_rows):
    per = tm // HALO
    last = n_rows // HALO - 1
    prev = pl.BlockSpec((HALO, d_model), lambda i: (jnp.maximum(i * per - 1, 0), 0))
    nxt = pl.BlockSpec((HALO, d_model), lambda i: (jnp.minimum((i + 1) * per, last), 0))
    return prev, nxt


def _halo_ok(i, nct, ntiles):
    prev_ok = jnp.logical_and(i != 0, i != nct)
    next_ok = jnp.logical_and(i != nct - 1, i != ntiles - 1)
    return prev_ok, next_ok


def _halo_row_mask(i, tm, nct, ntiles):
    prev_ok, next_ok = _halo_ok(i, nct, ntiles)
    lo = jnp.where(prev_ok, 0, HALO)
    hi = jnp.where(next_ok, tm + 2 * HALO, tm + HALO)
    rows = lax.broadcasted_iota(jnp.int32, (tm + 2 * HALO, 1), 0)
    return jnp.logical_and(rows >= lo, rows < hi).astype(F32)


def _mod_kernel(cT_ref, w_ref, b_ref, o_ref):
    s = _silu(cT_ref[...])
    w = w_ref[0]
    rows = [jnp.sum(w * s[:, r:r + 1], axis=0, keepdims=True) for r in range(2)]
    pad = jnp.zeros((o_ref.shape[1] - 2, w.shape[1]), F32)
    o_ref[0] = jnp.concatenate(rows + [pad], axis=0) + b_ref[0]


def _mod_vectors(c, c_ctx, mod_w, mod_b):
    depth, d_model, d3 = mod_w.shape
    cpad = jnp.zeros((d_model, 8), F32).at[:, 0].set(c[0]).at[:, 1].set(c_ctx)
    out = pl.pallas_call(
        _mod_kernel,
        grid=(depth, MOD_COL_BLOCKS),
        in_specs=[
            pl.BlockSpec((d_model, 8), lambda i, j: (0, 0)),
            pl.BlockSpec((1, d_model, d3 // MOD_COL_BLOCKS), lambda i, j: (i, 0, j)),
            pl.BlockSpec((1, 1, d3 // MOD_COL_BLOCKS), lambda i, j: (i, 0, j)),
        ],
        out_specs=pl.BlockSpec((1, 8, d3 // MOD_COL_BLOCKS), lambda i, j: (i, 0, j)),
        out_shape=jax.ShapeDtypeStruct((depth, 8, d3), F32),
        name="mod_vectors",
    )(cpad, mod_w, mod_b.reshape(depth, 1, d3))
    return out


def _mamba_inproj_kernel(xp_ref, xm_ref, xn_ref, mod_ref, wz_ref, wxbc_ref, wdt_ref, cw_ref, cb_ref,
                         dtb_ref, z_ref, xbc_ref, dt_ref, dtT_ref, xs_scr, out_scr, *, tm, nct, ntiles):
    i = pl.program_id(0)
    m = mod_ref[0]
    shift, scale = m[0:1], m[1:2]

    def modulate(v):
        return v * (1.0 + scale) + shift

    hx_m = modulate(xm_ref[...])
    hx_mb = hx_m.astype(BF16)
    pad = SSM_CONV // 2

    nph = HALO
    per = tm // nph
    blk = per + HALO
    prev_ok, next_ok = _halo_ok(i, nct, ntiles)
    lead_row = lax.broadcasted_iota(jnp.int32, (HALO, 1), 0)
    next_row = jnp.where(next_ok, 0, -1)
    prev_row = jnp.where(prev_ok, HALO - 1, -1)
    n_slab = xm_ref.shape[1] // LANES
    for l in range(n_slab):
        xs_scr[l] = hx_m[:, l * LANES:(l + 1) * LANES]
    pieces = []
    for b in range(nph):
        lead = jnp.where(lead_row == next_row, modulate(xn_ref[b:b + 1, :]),
                         jnp.where(lead_row == prev_row, modulate(xp_ref[b:b + 1, :]), 0.0))
        body = jnp.concatenate([xs_scr[l, pl.ds(b, per, stride=nph), :] for l in range(n_slab)], axis=1)
        pieces += [lead, body]
    hx_perm = jnp.concatenate(pieces, axis=0).astype(BF16)

    n_xbc = wxbc_ref.shape[1]
    for c in range(n_xbc // COL_CHUNK):
        cs = slice(c * COL_CHUNK, (c + 1) * COL_CHUNK)
        pre = jnp.dot(hx_perm, wxbc_ref[:, cs], preferred_element_type=F32)
        blocks = {b: pre[b * blk:(b + 1) * blk] for b in range(nph)}
        for b in range(pad):
            blocks[b + nph] = pltpu.roll(blocks[b], blk - 1, 0)
            blocks[-1 - b] = pltpu.roll(blocks[nph - 1 - b], 1, 0)
        for b in range(nph):
            acc = cb_ref[:, cs] + cw_ref[0:1, cs] * blocks[b - pad][HALO:]
            for k in range(1, SSM_CONV):
                acc = acc + cw_ref[k:k + 1, cs] * blocks[b + k - pad][HALO:]
            act = _silu(acc)
            for l in range(COL_CHUNK // LANES):
                out_scr[c % 2, l, pl.ds(b, per, stride=nph), :] = act[:, l * LANES:(l + 1) * LANES]
        xbc_ref[:, cs] = jnp.concatenate([out_scr[c % 2, l] for l in range(COL_CHUNK // LANES)],
                                         axis=1).astype(BF16)

    n_z = wz_ref.shape[1]
    for c in range(n_z // COL_CHUNK):
        cs = slice(c * COL_CHUNK, (c + 1) * COL_CHUNK)
        z_ref[:, cs] = jnp.dot(hx_mb, wz_ref[:, cs], preferred_element_type=F32).astype(BF16)

    dt_raw = jnp.dot(hx_m, wdt_ref[...], precision=HIGHEST, preferred_element_type=F32) + dtb_ref[...]
    dt = jnp.maximum(dt_raw, 0.0) + jnp.log1p(jnp.exp(-jnp.abs(dt_raw)))
    dt_ref[...] = dt
    dtT_ref[...] = dt.T


def _mamba_inproj(X, mod2, w_in, conv_w, conv_b, dt_bias, nct_rows):
    n_rows, d_model = X.shape
    tm = ROW_TILE
    ntiles, nct = n_rows // tm, nct_rows // tm
    d_inner = w_in.shape[1] - conv_w.shape[1] - 2 * (dt_bias.shape[-1])
    n_xbc = conv_w.shape[1]
    n_dt = 2 * dt_bias.shape[-1]
    wz = w_in[:, :d_inner].astype(BF16)
    wxbc = w_in[:, d_inner:d_inner + n_xbc].astype(BF16)
    wdt = jnp.zeros((d_model, LANES), F32).at[:, :n_dt].set(w_in[:, d_inner + n_xbc:])
    dtb = jnp.zeros((1, LANES), F32).at[0, :n_dt].set(dt_bias.reshape(-1))
    cw = jnp.zeros((8, n_xbc), F32).at[:SSM_CONV].set(conv_w)
    prev, nxt = _halo_specs(d_model, tm, n_rows)
    kern = functools.partial(_mamba_inproj_kernel, tm=tm, nct=nct, ntiles=ntiles)
    return pl.pallas_call(
        kern,
        grid=(ntiles,),
        in_specs=[
            prev,
            pl.BlockSpec((tm, d_model), lambda i: (i, 0)),
            nxt,
            _mod_spec(d_model, nct),
            _resident(wz.shape), _resident(wxbc.shape), _resident(wdt.shape),
            _resident(cw.shape), _resident((1, n_xbc)), _resident(dtb.shape),
        ],
        out_specs=[
            pl.BlockSpec((tm, d_inner), lambda i: (i, 0)),
            pl.BlockSpec((tm, n_xbc), lambda i: (i, 0)),
            pl.BlockSpec((tm, LANES), lambda i: (i, 0)),
            pl.BlockSpec((LANES, tm), lambda i: (0, i)),
        ],
        out_shape=[
            jax.ShapeDtypeStruct((n_rows, d_inner), BF16),
            jax.ShapeDtypeStruct((n_rows, n_xbc), BF16),
            jax.ShapeDtypeStruct((n_rows, LANES), F32),
            jax.ShapeDtypeStruct((LANES, n_rows), F32),
        ],
        scratch_shapes=[pltpu.VMEM((d_model // LANES, tm, LANES), F32),
                        pltpu.VMEM((2, COL_CHUNK // LANES, tm, LANES), F32)],
        name="mamba_inproj",
    )(X, X, X, mod2, wz, wxbc, wdt, cw, conv_b.reshape(1, n_xbc), dtb)


def _split3(v):
    hi = v.astype(BF16)
    r1 = v - hi.astype(F32)
    mid = r1.astype(BF16)
    lo = (r1 - mid.astype(F32)).astype(BF16)
    return hi, mid, lo


def _ssd_direction(d, xs_ref, b_ref, c_ref, dt_ref, dtT_ref, arow_ref, acol_ref, r2_ref, y_ref,
                   state_scr, lower, upper, lane_head, *, L, n_heads):
    n_groups = SSM_GROUPS
    hpg = n_heads // n_groups
    gw = hpg * SSM_HEAD_DIM
    ns = SSM_STATE
    dt = dt_ref[...]
    dA = dt * arow_ref[0:1, :]
    dAT = dtT_ref[...] * acol_ref[...]
    low3, up3 = (jnp.concatenate([t.astype(BF16)] * 3, axis=1) for t in (lower, upper))
    low3v, up3v = (jnp.concatenate([t.astype(BF16)] * 3, axis=0) for t in (lower, upper))
    dA3 = jnp.concatenate(_split3(dA), axis=0)
    dAT3 = jnp.concatenate(_split3(dAT), axis=1)
    if d == 0:
        cum = jnp.dot(low3, dA3, preferred_element_type=F32)
        cumT = jnp.dot(dAT3, up3v, preferred_element_type=F32)
        tot = cum[L - 1:L]
        mask = lower
    else:
        cum = jnp.dot(up3, dA3, preferred_element_type=F32)
        cumT = jnp.dot(dAT3, low3v, preferred_element_type=F32)
        tot = cum[0:1]
        mask = upper
    e_in = jnp.exp2(cum)
    tail = jnp.exp2(tot - cum)
    st = jnp.concatenate([dt * tail, e_in], axis=0)
    hi = st.astype(BF16)
    lo = (st - hi.astype(F32)).astype(BF16)
    lhs = jnp.concatenate([hi, lo], axis=1)
    src_row = cumT - jnp.log2(dtT_ref[...])
    neg_inf = jnp.float32(-jnp.inf)

    for g in range(n_groups):
        gs = slice(g * gw, (g + 1) * gw)
        rep = jnp.dot(lhs, r2_ref[:, gs], preferred_element_type=F32)
        xs_b = xs_ref[:, gs]
        xdt_tail = (xs_b.astype(F32) * rep[0:L]).astype(BF16)
        e_rep = rep[L:2 * L]
        bg = b_ref[:, g * ns:(g + 1) * ns]
        cg = c_ref[:, g * ns:(g + 1) * ns]
        cb = lax.dot_general(cg, bg, (((1,), (1,)), ((), ())), preferred_element_type=F32)
        bs = L // SSD_TRI_BLOCKS
        xks = [jnp.where(lane_head == k, xs_b, jnp.zeros_like(xs_b)) for k in range(hpg)]
        y_blocks = []
        for i in range(SSD_TRI_BLOCKS):
            rows = slice(i * bs, (i + 1) * bs)
            cols = slice(0, (i + 1) * bs) if d == 0 else slice(i * bs, L)
            scs = []
            for k in range(hpg):
                lane = d * n_heads + g * hpg + k
                seg = cum[rows, lane:lane + 1] - src_row[lane:lane + 1, cols]
                dec_dt = jnp.exp2(jnp.where(mask[rows, cols], seg, neg_inf))
                scs.append((cb[rows, cols] * dec_dt).astype(BF16))
            y_blocks.append(jnp.dot(jnp.concatenate(scs, axis=1), jnp.concatenate([x[cols] for x in xks], axis=0),
                                    preferred_element_type=F32))
        y = jnp.concatenate(y_blocks, axis=0)
        h_t = state_scr[d, g]
        y_off = jnp.dot(cg, h_t.astype(BF16), preferred_element_type=F32) * e_rep
        e_tot = e_rep[L - 1:L] if d == 0 else e_rep[0:1]
        upd = lax.dot_general(bg, xdt_tail, (((0,), (0,)), ((), ())), preferred_element_type=F32)
        state_scr[d, g] = h_t * e_tot + upd
        y_ref[:, gs] = (y + y_off).astype(BF16)


def _ssd_kernel(xsf, bf, cf, dtf, dtTf, xsb, bb, cb_, dtb, dtTb, arow_ref, acol_ref, r2f_ref, r2b_ref,
                yf_ref, yb_ref, state_scr, *, L, n_heads):
    @pl.when(pl.program_id(0) == 0)
    def _():
        state_scr[...] = jnp.zeros_like(state_scr)

    r = lax.broadcasted_iota(jnp.int32, (L, L), 0)
    c = lax.broadcasted_iota(jnp.int32, (L, L), 1)
    lower, upper = c <= r, r <= c
    gw = (n_heads // SSM_GROUPS) * SSM_HEAD_DIM
    lane_head = lax.broadcasted_iota(jnp.int32, (1, gw), 1) // SSM_HEAD_DIM
    common = dict(L=L, n_heads=n_heads)
    _ssd_direction(0, xsf, bf, cf, dtf, dtTf, arow_ref, acol_ref, r2f_ref, yf_ref, state_scr,
                   lower, upper, lane_head, **common)
    _ssd_direction(1, xsb, bb, cb_, dtb, dtTb, arow_ref, acol_ref, r2b_ref, yb_ref, state_scr,
                   lower, upper, lane_head, **common)


def _ssd_scan(xbc, dt, dtT, a_log, d_inner):
    n_rows = xbc.shape[0]
    L = SSM_CHUNK
    n = n_rows // L
    n_heads = a_log.shape[-1]
    n_bc = SSM_GROUPS * SSM_STATE
    a = -jnp.exp(a_log.astype(F32)).reshape(-1) * LOG2_E
    a_pad = jnp.zeros((LANES,), F32).at[:2 * n_heads].set(a)
    arow = jnp.zeros((8, LANES), F32).at[0].set(a_pad)
    acol = jnp.broadcast_to(a_pad[:, None], (LANES, L))
    head_of_col = jnp.arange(d_inner) // SSM_HEAD_DIM
    r2 = []
    for d in range(2):
        sel = (jnp.arange(LANES)[:, None] == (d * n_heads + head_of_col)[None, :]).astype(BF16)
        r2.append(jnp.concatenate([sel, sel], axis=0))
    fwd = lambda i: i
    bwd = lambda i: jnp.where(i == 0, 0, n - i)
    assert n_rows // L >= 2 and L == ROW_TILE

    def specs(cm):
        return [
            pl.BlockSpec((L, d_inner), lambda i: (cm(i), 0)),
            pl.BlockSpec((L, n_bc), lambda i: (cm(i), d_inner // n_bc)),
            pl.BlockSpec((L, n_bc), lambda i: (cm(i), d_inner // n_bc + 1)),
            pl.BlockSpec((L, LANES), lambda i: (cm(i), 0)),
            pl.BlockSpec((LANES, L), lambda i: (0, cm(i))),
        ]

    kern = functools.partial(_ssd_kernel, L=L, n_heads=n_heads)
    return pl.pallas_call(
        kern,
        grid=(n,),
        in_specs=specs(fwd) + specs(bwd) + [
            _resident(arow.shape), _resident(acol.shape), _resident(r2[0].shape), _resident(r2[1].shape)],
        out_specs=[
            pl.BlockSpec((L, d_inner), lambda i: (fwd(i), 0)),
            pl.BlockSpec((L, d_inner), lambda i: (bwd(i), 0)),
        ],
        out_shape=[jax.ShapeDtypeStruct((n_rows, d_inner), BF16)] * 2,
        scratch_shapes=[pltpu.VMEM((2, SSM_GROUPS, SSM_STATE, d_inner // SSM_GROUPS), F32)],
        compiler_params=pltpu.CompilerParams(dimension_semantics=("arbitrary",)),
        name="ssd_scan",
    )(xbc, xbc, xbc, dt, dtT, xbc, xbc, xbc, dt, dtT, arow, acol, r2[0], r2[1])


def _mamba_finish_kernel(yf_ref, yb_ref, xs_ref, z_ref, x_ref, mod_ref, drep_ref, ng_ref, wout_ref,
                         lng_ref, lnb_ref, o_ref):
    y = yf_ref[...].astype(F32) + yb_ref[...].astype(F32) + xs_ref[...].astype(F32) * drep_ref[...]
    t = y * _silu(z_ref[...].astype(F32))
    t = t * lax.rsqrt(jnp.mean(t * t, axis=-1, keepdims=True) + RMS_EPS) * ng_ref[...]
    out = jnp.dot(t.astype(BF16), wout_ref[...], preferred_element_type=F32)
    gate = mod_ref[0][2:3]
    o_ref[...] = _layer_norm(DEEPNORM_ALPHA * x_ref[...] + gate * out, lng_ref[...], lnb_ref[...])


def _mamba_finish(yf, yb, xbc, z, X, mod2, d_skip, norm_g, w_out, ln_g, ln_b, nct_rows, latent_only):
    n_rows, d_model = X.shape
    d_inner = z.shape[1]
    tm = ROW_TILE
    ntiles, nct = n_rows // tm, nct_rows // tm
    skip = nct if latent_only else 0
    drep = jnp.repeat(d_skip.astype(F32), SSM_HEAD_DIM).reshape(1, d_inner)
    row = lambda w: pl.BlockSpec((tm, w), lambda i: (i + skip, 0))
    return pl.pallas_call(
        _mamba_finish_kernel,
        grid=(ntiles - skip,),
        in_specs=[row(d_inner), row(d_inner), row(d_inner), row(d_inner), row(d_model),
                  _mod_spec(d_model, nct - skip), _resident((1, d_inner)), _resident((1, d_inner)),
                  _resident(w_out.shape), _resident((1, d_model)), _resident((1, d_model))],
        out_specs=pl.BlockSpec((tm, d_model), lambda i: (i, 0)),
        out_shape=jax.ShapeDtypeStruct((n_rows - skip * tm, d_model), F32),
        name="mamba_finish",
    )(yf, yb, xbc, z, X, mod2, drep, norm_g.reshape(1, d_inner), w_out.astype(BF16),
      ln_g.reshape(1, d_model), ln_b.reshape(1, d_model))


def _mamba_layer(X, mod2, w_in, conv_w, conv_b, dt_bias, a_log, d_skip, norm_g, w_out, ln_g, ln_b, nct_rows,
                 latent_only):
    z, xbc, dt, dtT = _mamba_inproj(X, mod2, w_in, conv_w, conv_b, dt_bias, nct_rows)
    yf, yb = _ssd_scan(xbc, dt, dtT, a_log, z.shape[1])
    return _mamba_finish(yf, yb, xbc, z, X, mod2, d_skip, norm_g, w_out, ln_g, ln_b, nct_rows, latent_only)


def _attn_inproj_kernel(x_ref, mod_ref, w_ref, gm_ref, qn_ref, kn_ref, cos_ref, sin_ref,
                        qT_ref, k_ref, vT_ref, g_ref, *, wq, wkv):
    m = mod_ref[0]
    hx = (x_ref[...] * (1.0 + m[1:2]) + m[0:1]).astype(BF16)
    hd = ATTN_HEAD_DIM

    def norm_rope(t, gmean, gain):
        width = t.shape[1]
        ms = jnp.dot((t * t).astype(BF16), gmean, preferred_element_type=F32)
        tn = t * lax.rsqrt(ms + RMS_EPS) * gain
        reps = width // LANES
        cs = jnp.concatenate([cos_ref[...]] * reps, axis=1)
        sn = jnp.concatenate([sin_ref[...]] * reps, axis=1)
        lane = lax.broadcasted_iota(jnp.int32, (1, width), 1)
        first_half = (lane % hd) < hd // 2
        swapped = jnp.where(first_half, pltpu.roll(tn, width - hd // 2, 1), pltpu.roll(tn, hd // 2, 1))
        return tn * cs + swapped * sn

    q = jnp.dot(hx, w_ref[:, 0:wq], preferred_element_type=F32)
    qr = norm_rope(q, gm_ref[...], qn_ref[...]) * (hd ** -0.5 * LOG2_E)
    qT_ref[...] = qr.T.astype(BF16)
    k = jnp.dot(hx, w_ref[:, wq:wq + wkv], preferred_element_type=F32)
    kr = norm_rope(k, gm_ref[0:wkv, 0:wkv], kn_ref[...])
    for j in range(wkv // hd):
        k_ref[j] = kr[:, j * hd:(j + 1) * hd].astype(BF16)
    v = jnp.dot(hx, w_ref[:, wq + wkv:wq + 2 * wkv], preferred_element_type=F32)
    v_t = v.T.astype(BF16)
    for j in range(wkv // hd):
        vT_ref[j, 0:hd, :] = v_t[j * hd:(j + 1) * hd, :]
        vT_ref[j, hd:, :] = jnp.ones((V_ROWS - hd, v_t.shape[1]), BF16)
    g_ref[...] = jnp.dot(hx, w_ref[:, wq + 2 * wkv:], preferred_element_type=F32).astype(BF16)


def _rope_tables(n_tok, n_ctx):
    t = jnp.arange(n_tok)
    row_ids = (t // GRID_W).astype(F32)
    col_ids = (t % GRID_W).astype(F32)
    half = ATTN_HEAD_DIM // 2
    inv = ROPE_THETA ** (-jnp.arange(0, half, 2, dtype=F32) / half)
    ang = jnp.concatenate([row_ids[:, None] * inv, col_ids[:, None] * inv], axis=-1)
    c, s = jnp.cos(ang), jnp.sin(ang)
    cos_h = jnp.concatenate([c, c], axis=-1)
    sin_h = jnp.concatenate([-s, s], axis=-1)
    reps = LANES // ATTN_HEAD_DIM
    cos2 = jnp.concatenate([jnp.ones((n_ctx, LANES), F32), jnp.tile(cos_h, (1, reps))], axis=0)
    sin2 = jnp.concatenate([jnp.zeros((n_ctx, LANES), F32), jnp.tile(sin_h, (1, reps))], axis=0)
    return cos2, sin2


def _attn_inproj(X, mod2, w_in, q_norm, k_norm, cos2, sin2, nct_rows):
    n_rows, d_model = X.shape
    tm = ROW_TILE
    ntiles, nct = n_rows // tm, nct_rows // tm
    hd = ATTN_HEAD_DIM
    wq, wkv = ATTN_HEADS * hd, ATTN_KV_HEADS * hd
    idx = jnp.arange(wq) // hd
    gmean = (idx[:, None] == idx[None, :]).astype(BF16) * (1.0 / hd)
    qn = jnp.tile(q_norm.astype(F32), ATTN_HEADS).reshape(1, wq)
    kn = jnp.tile(k_norm.astype(F32), ATTN_KV_HEADS).reshape(1, wkv)
    w = w_in.astype(BF16)
    kern = functools.partial(_attn_inproj_kernel, wq=wq, wkv=wkv)
    return pl.pallas_call(
        kern,
        grid=(ntiles,),
        in_specs=[
            pl.BlockSpec((tm, d_model), lambda i: (i, 0)),
            _mod_spec(d_model, nct),
            _resident(w.shape), _resident(gmean.shape), _resident(qn.shape), _resident(kn.shape),
            pl.BlockSpec((tm, LANES), lambda i: (i, 0)),
            pl.BlockSpec((tm, LANES), lambda i: (i, 0)),
        ],
        out_specs=[
            pl.BlockSpec((wq, tm), lambda i: (0, i)),
            pl.BlockSpec((ATTN_KV_HEADS, tm, hd), lambda i: (0, i, 0)),
            pl.BlockSpec((ATTN_KV_HEADS, V_ROWS, tm), lambda i: (0, 0, i)),
            pl.BlockSpec((tm, wq), lambda i: (i, 0)),
        ],
        out_shape=[
            jax.ShapeDtypeStruct((wq, n_rows), BF16),
            jax.ShapeDtypeStruct((ATTN_KV_HEADS, n_rows, hd), BF16),
            jax.ShapeDtypeStruct((ATTN_KV_HEADS, V_ROWS, n_rows), BF16),
            jax.ShapeDtypeStruct((n_rows, wq), BF16),
        ],
        name="attn_inproj",
    )(X, mod2, w, gmean, qn, kn, cos2, sin2)


def _flash_kernel(qT_ref, k_ref, vT_ref, o_ref, s_scr, m_scr, acc_scr, *, bk, nct_tiles, nct_chunks, n_chunks,
                  bounded):
    qi = pl.program_id(1)
    hd = ATTN_HEAD_DIM
    n_rep = ATTN_HEADS // ATTN_KV_HEADS
    m_scr[...] = jnp.full_like(m_scr, -jnp.inf)
    acc_scr[...] = jnp.zeros_like(acc_scr)
    nch = jnp.where(qi < nct_tiles, nct_chunks, n_chunks)

    def produce(buf, j):
        off = pl.multiple_of(jnp.minimum(j, nch - 1) * bk, bk)
        kb = k_ref[0, pl.ds(off, bk), :]
        for g in range(n_rep):
            s_scr[buf, g] = jnp.dot(kb, qT_ref[g * hd:(g + 1) * hd, :], preferred_element_type=F32)

    def consume(buf, j):
        off = pl.multiple_of(j * bk, bk)
        vb = vT_ref[0, :, pl.ds(off, bk)]
        for g in range(n_rep):
            s = s_scr[buf, g]
            if bounded:
                acc_scr[g] += jnp.dot(vb, jnp.exp2(s).astype(BF16), preferred_element_type=F32)
                continue
            m_old = m_scr[g]
            m_new = jnp.maximum(m_old, jnp.max(s, axis=0, keepdims=True))
            alpha = jnp.exp2(m_old - m_new)
            p = jnp.exp2(s - m_new).astype(BF16)
            acc_scr[g] = alpha * acc_scr[g] + jnp.dot(vb, p, preferred_element_type=F32)
            m_scr[g] = m_new

    last = nch - 1
    produce(0, last)
    produce(1, 0)
    consume(0, last)
    unroll = FLASH_UNROLL_BOUNDED if bounded else FLASH_UNROLL
    n_trips = last // unroll

    def trip(t, carry):
        for u in range(unroll):
            j = unroll * t + u
            produce(u % 2, j + 1)
            consume((u + 1) % 2, j)
        return carry

    lax.fori_loop(0, n_trips, trip, 0)

    def leftover(j, carry):
        produce(0, j)
        consume(0, j)
        return carry

    lax.fori_loop(n_trips * unroll, last, leftover, 0)

    o_t = jnp.concatenate([acc_scr[g, 0:hd] / acc_scr[g, hd:hd + 1] for g in range(n_rep)], axis=0)
    o_ref[...] = o_t.T.astype(BF16)


def _flash_attention(qT, k, vT, nct_rows, scores_bounded):
    wq, n_rows = qT.shape
    hd = ATTN_HEAD_DIM
    n_rep = ATTN_HEADS // ATTN_KV_HEADS
    bq = bk = ROW_TILE
    static = dict(bk=bk, nct_tiles=nct_rows // bq, nct_chunks=nct_rows // bk, n_chunks=n_rows // bk)

    def call(bounded, name):
        return pl.pallas_call(
            functools.partial(_flash_kernel, bounded=bounded, **static),
            grid=(ATTN_KV_HEADS, n_rows // bq),
            in_specs=[
                pl.BlockSpec((n_rep * hd, bq), lambda h, i: (h, i)),
                pl.BlockSpec((1, n_rows, hd), lambda h, i: (h, 0, 0)),
                pl.BlockSpec((1, V_ROWS, n_rows), lambda h, i: (h, 0, 0)),
            ],
            out_specs=pl.BlockSpec((bq, n_rep * hd), lambda h, i: (i, h)),
            out_shape=jax.ShapeDtypeStruct((n_rows, wq), BF16),
            scratch_shapes=[
                pltpu.VMEM((2, n_rep, bk, bq), F32),
                pltpu.VMEM((n_rep, 1, bq), F32),
                pltpu.VMEM((n_rep, V_ROWS, bq), F32),
            ],
            name=name,
        )

    return lax.cond(scores_bounded, call(True, "flash_attention_bounded"), call(False, "flash_attention"),
                    qT, k, vT)


def _attn_out_kernel(o_ref, g_ref, x_ref, mod_ref, wout_ref, lng_ref, lnb_ref, out_ref):
    t = o_ref[...].astype(F32) * _silu(g_ref[...].astype(F32))
    out = jnp.dot(t.astype(BF16), wout_ref[...], preferred_element_type=F32)
    gate = mod_ref[0][2:3]
    out_ref[...] = _layer_norm(DEEPNORM_ALPHA * x_ref[...] + gate * out, lng_ref[...], lnb_ref[...])


def _attn_out(o, g, X, mod2, w_out, ln_g, ln_b, nct_rows):
    n_rows, d_model = X.shape
    wq = o.shape[1]
    tm = ROW_TILE
    ntiles, nct = n_rows // tm, nct_rows // tm
    row = lambda w: pl.BlockSpec((tm, w), lambda i: (i, 0))
    return pl.pallas_call(
        _attn_out_kernel,
        grid=(ntiles,),
        in_specs=[row(wq), row(wq), row(d_model), _mod_spec(d_model, nct), _resident(w_out.shape),
                  _resident((1, d_model)), _resident((1, d_model))],
        out_specs=row(d_model),
        out_shape=jax.ShapeDtypeStruct((n_rows, d_model), F32),
        name="attn_out",
    )(o, g, X, mod2, w_out.astype(BF16), ln_g.reshape(1, d_model), ln_b.reshape(1, d_model))


def _attn_layer(X, mod2, w_in, q_norm, k_norm, w_out, ln_g, ln_b, cos2, sin2, nct_rows):
    qT, k, vT, g = _attn_inproj(X, mod2, w_in, q_norm, k_norm, cos2, sin2, nct_rows)
    score_bound = LOG2_E * ATTN_HEAD_DIM ** 0.5 * jnp.max(jnp.abs(q_norm)) * jnp.max(jnp.abs(k_norm))
    o = _flash_attention(qT, k, vT, nct_rows, score_bound <= SAFE_LOG2_SCORE)
    return _attn_out(o, g, X, mod2, w_out, ln_g, ln_b, nct_rows)


def _pool_kernel(xp_ref, xm_ref, xn_ref, mod_ref, win_ref, gw_ref, ls_ref, wout_ref, lng_ref, lnb_ref,
                 o_ref, ext_scr, t_scr, *, tm, nct, ntiles, n_ctx, n_tok):
    i = pl.program_id(0)
    m = mod_ref[0]
    shift, scale, gate = m[0:1], m[1:2], m[2:3]

    def modulate(v):
        return v * (1.0 + scale) + shift

    x_m = xm_ref[...]
    hx_m = modulate(x_m)
    hx_ext = jnp.concatenate([modulate(xp_ref[...]), hx_m, modulate(xn_ref[...])], axis=0).astype(BF16)
    hx_mb = hx_m.astype(BF16)
    maskf = _halo_row_mask(i, tm, nct, ntiles)

    is_ctx = i < nct
    t0 = (i - jnp.where(is_ctx, 0, nct)) * tm
    n_seq = jnp.where(is_ctx, n_ctx, n_tok)
    pos = t0 + lax.broadcasted_iota(jnp.int32, (tm, 1), 0)

    n_groups = len(POOL_WINDOWS)
    gd = gw_ref.shape[1]
    width = n_groups * gd
    for gi, window in enumerate(POOL_WINDOWS):
        cs = slice(gi * gd, (gi + 1) * gd)
        ext_scr[...] = jnp.dot(hx_ext, win_ref[:, cs], preferred_element_type=F32) * maskf
        lo_off = window // 2
        hi_off = window - 1 - lo_off
        win_sum = ext_scr[pl.ds(HALO - lo_off, tm), :]
        for o in range(-lo_off + 1, hi_off + 1):
            win_sum = win_sum + ext_scr[pl.ds(HALO + o, tm), :]
        cnt = (jnp.minimum(pos + hi_off + 1, n_seq) - jnp.maximum(pos - lo_off, 0)).astype(F32)
        pooled = win_sum / cnt - ext_scr[pl.ds(HALO, tm), :]
        mixed = jnp.dot(pooled.astype(BF16), gw_ref[gi], preferred_element_type=F32)
        zc = jnp.dot(hx_mb, win_ref[:, width + gi * gd:width + (gi + 1) * gd], preferred_element_type=F32)
        t_scr[:, cs] = ((mixed * ls_ref[:, cs]) * _silu(zc)).astype(BF16)
    out = jnp.dot(t_scr[...], wout_ref[...], preferred_element_type=F32)
    o_ref[...] = _layer_norm(DEEPNORM_ALPHA * x_m + gate * out, lng_ref[...], lnb_ref[...])


def _pool_layer(X, mod2, w_in, group_w, layer_scale, w_out, ln_g, ln_b, nct_rows):
    n_rows, d_model = X.shape
    tm = ROW_TILE
    ntiles, nct = n_rows // tm, nct_rows // tm
    width = w_out.shape[0]
    gd = group_w.shape[1]
    assert max(POOL_WINDOWS) // 2 <= HALO
    prev, nxt = _halo_specs(d_model, tm, n_rows)
    kern = functools.partial(_pool_kernel, tm=tm, nct=nct, ntiles=ntiles, n_ctx=nct_rows,
                             n_tok=n_rows - nct_rows)
    return pl.pallas_call(
        kern,
        grid=(ntiles,),
        in_specs=[
            prev,
            pl.BlockSpec((tm, d_model), lambda i: (i, 0)),
            nxt,
            _mod_spec(d_model, nct),
            _resident(w_in.shape), _resident(group_w.shape), _resident((1, width)), _resident(w_out.shape),
            _resident((1, d_model)), _resident((1, d_model)),
        ],
        out_specs=pl.BlockSpec((tm, d_model), lambda i: (i, 0)),
        out_shape=jax.ShapeDtypeStruct((n_rows, d_model), F32),
        scratch_shapes=[pltpu.VMEM((tm + 2 * HALO, gd), F32), pltpu.VMEM((tm, width), BF16)],
        name="pool_mixer",
    )(X, X, X, mod2, w_in.astype(BF16), group_w.astype(BF16), layer_scale.reshape(1, width),
      w_out.astype(BF16), ln_g.reshape(1, d_model), ln_b.reshape(1, d_model))


def kernel(x, c, ctx, c_ctx, mod_w, mod_b, ln_g, ln_b, ssm_w_in, ssm_conv_w, ssm_conv_b, ssm_dt_bias,
           ssm_a_log, ssm_d, ssm_norm_g, ssm_w_out, attn_w_in, attn_q_norm, attn_k_norm, attn_w_out,
           pool_w_in, pool_group_w, pool_scale, pool_w_out):
    bsz, n_tok, d_model = x.shape
    n_ctx = ctx.shape[1]
    assert bsz == 1 and n_ctx % ROW_TILE == 0 and n_tok % ROW_TILE == 0
    X = jnp.concatenate([ctx[0], x[0]], axis=0)
    mods = _mod_vectors(c, c_ctx, mod_w, mod_b)
    cos2, sin2 = _rope_tables(n_tok, n_ctx)
    for i in range(DEPTH):
        kind, j = i % N_MIXERS, i // N_MIXERS
        m3 = mods[i, 0:2].reshape(2, 3, d_model)
        mod2 = jnp.zeros((2, 8, d_model), F32).at[:, 0:3].set(m3)
        if kind == 0:
            X = _mamba_layer(X, mod2, ssm_w_in[j], ssm_conv_w[j], ssm_conv_b[j], ssm_dt_bias[j],
                             ssm_a_log[j], ssm_d[j], ssm_norm_g[j], ssm_w_out[j], ln_g[i], ln_b[i], n_ctx,
                             latent_only=(i == DEPTH - 1))
        elif kind == 1:
            X = _attn_layer(X, mod2, attn_w_in[j], attn_q_norm[j], attn_k_norm[j], attn_w_out[j],
                            ln_g[i], ln_b[i], cos2, sin2, n_ctx)
        else:
            X = _pool_layer(X, mod2, pool_w_in[j], pool_group_w[j], pool_scale[j], pool_w_out[j],
                            ln_g[i], ln_b[i], n_ctx)
    if (DEPTH - 1) % N_MIXERS != 0:
        X = X[n_ctx:]
    return X[None]
```
